```python
import math
import jax, jax.numpy as jnp
from jax import lax
import numpy as np

D_MODEL = 1024
BATCH = 8
SEQ = 2048
DEPTH = 2
DEC_BATCH = 4
DEC_SEQ = 4096
PAST_LEN = 128

MIX_WIDTH = D_MODEL
HEAD_DIM = 64
DIFF_WIDTH = MIX_WIDTH // 2
DIFF_HEADS = DIFF_WIDTH // (2 * HEAD_DIM)
QK_WIDTH = DIFF_HEADS * 2 * HEAD_DIM
V_WIDTH = DIFF_WIDTH
FOURIER_WIDTH = MIX_WIDTH - DIFF_WIDTH
FOURIER_GROUP_DIM = 64
FOURIER_GROUPS = FOURIER_WIDTH // FOURIER_GROUP_DIM
IN_COLS = 2 * QK_WIDTH + V_WIDTH + FOURIER_WIDTH
D_FF = 2816
NUM_BUCKETS = 32
REL_MAX_DISTANCE = 128
Q_BLOCK = 128
N_MOD = 9
EPS = 1e-6
ATTN_SCALE = HEAD_DIM ** -0.5

kernel_name = "hybrid_diffattn_fnet_macaron_encoder"


def rmsnorm(x, g):
    xf = x.astype(jnp.float32)
    y = xf * lax.rsqrt(jnp.mean(xf * xf, axis=-1, keepdims=True) + EPS)
    return (y * g.astype(jnp.float32)).astype(x.dtype)


def modulate(h, shift, scale):
    return h * (1 + scale[:, None, :]) + shift[:, None, :]


def swiglu(h, wi, wo):
    gu = h @ wi
    g, u = jnp.split(gu, 2, axis=-1)
    return (jax.nn.silu(g) * u) @ wo


def rel_bucket(rel):
    nb = NUM_BUCKETS // 2
    max_exact = nb // 2
    ret = (rel > 0).astype(jnp.int32) * nb
    n = jnp.abs(rel)
    nf = jnp.maximum(n, 1).astype(jnp.float32)
    large = max_exact + (jnp.log(nf / max_exact) / math.log(REL_MAX_DISTANCE / max_exact)
                         * (nb - max_exact)).astype(jnp.int32)
    large = jnp.minimum(large, nb - 1)
    return ret + jnp.where(n < max_exact, n, large)


def diff_attention(q, k, v, lam, rel_bias):
    B, H, _, S, dh = q.shape
    n_blk = S // Q_BLOCK
    q_blocks = q.reshape(B, H, 2, n_blk, Q_BLOCK, dh).transpose(3, 0, 1, 2, 4, 5)
    starts = jnp.arange(n_blk, dtype=jnp.int32) * Q_BLOCK
    kpos = jnp.arange(S, dtype=jnp.int32)

    def block(args):
        qb, start = args
        qpos = start + jnp.arange(Q_BLOCK, dtype=jnp.int32)
        buckets = rel_bucket(kpos[None, :] - qpos[:, None])
        bias = rel_bias[buckets].transpose(2, 3, 0, 1)
        s = jnp.einsum('bhjqd,bhjkd->bhjqk', qb, k).astype(jnp.float32) * ATTN_SCALE
        s = s + bias.astype(jnp.float32)
        p = jax.nn.softmax(s, axis=-1)
        a = p[:, :, 0] - lam * p[:, :, 1]
        return jnp.einsum('bhqk,bhkd->bhqd', a.astype(v.dtype), v)

    o = lax.map(block, (q_blocks, starts))
    return o.transpose(1, 0, 3, 2, 4).reshape(B, S, H, 2 * dh)


def mixer(h, w_in, w_out, q_norm, k_norm, lambda_qk, subln, rel_bias, lambda_init):
    B, S, _ = h.shape
    z = h @ w_in
    q, k, v, f = jnp.split(z, [QK_WIDTH, 2 * QK_WIDTH, 2 * QK_WIDTH + V_WIDTH], axis=-1)
    q = rmsnorm(q.reshape(B, S, DIFF_HEADS, 2, HEAD_DIM), q_norm).transpose(0, 2, 3, 1, 4)
    k = rmsnorm(k.reshape(B, S, DIFF_HEADS, 2, HEAD_DIM), k_norm).transpose(0, 2, 3, 1, 4)
    v = v.reshape(B, S, DIFF_HEADS, 2 * HEAD_DIM).transpose(0, 2, 1, 3)
    lqk = lambda_qk.astype(jnp.float32)
    lam = (jnp.exp(jnp.sum(lqk[0] * lqk[1])) - jnp.exp(jnp.sum(lqk[2] * lqk[3]))
           + lambda_init)
    o = diff_attention(q, k, v, lam, rel_bias)
    o = (rmsnorm(o, subln) * (1.0 - lambda_init)).reshape(B, S, DIFF_WIDTH)
    fg = f.reshape(B, S, FOURIER_GROUPS, FOURIER_GROUP_DIM).astype(jnp.float32)
    fo = jnp.fft.fft2(fg, axes=(1, 3), norm='ortho').real.astype(h.dtype).reshape(B, S, FOURIER_WIDTH)
    return jnp.concatenate([o, fo], axis=-1) @ w_out


def trunk(x, c, ada_w, ada_b, norm_ffn1, norm_mix, norm_ffn2, ffn1_wi, ffn1_wo,
          ffn2_wi, ffn2_wo, w_in, w_out, q_norm, k_norm, lambda_qk, subln, rel_bias):
    sc = jax.nn.silu(c)
    for l in range(DEPTH):
        lambda_init = 0.8 - 0.6 * math.exp(-0.3 * l)
        mod = sc @ ada_w[l] + ada_b[l]
        sh1, s1, g1, sh2, s2, g2, sh3, s3, g3 = jnp.split(mod, N_MOD, axis=-1)
        h = modulate(rmsnorm(x, norm_ffn1[l]), sh1, s1)
        x = x + 0.5 * g1[:, None, :] * swiglu(h, ffn1_wi[l], ffn1_wo[l])
        h = modulate(rmsnorm(x, norm_mix[l]), sh2, s2)
        x = x + g2[:, None, :] * mixer(h, w_in[l], w_out[l], q_norm[l], k_norm[l],
                                       lambda_qk[l], subln[l], rel_bias, lambda_init)
        h = modulate(rmsnorm(x, norm_ffn2[l]), sh3, s3)
        x = x + 0.5 * g3[:, None, :] * swiglu(h, ffn2_wi[l], ffn2_wo[l])
    return x


def setup_inputs(seed: int = 0) -> dict:
    key = jax.random.key(seed)
    ks = jax.random.split(key, 24)
    f32 = jnp.float32
    nrm = lambda k, shape, s: jax.random.normal(k, shape, f32) * s
    gain = lambda k, shape: 1.0 + 0.02 * jax.random.normal(k, shape, f32)
    return {
        "x_prompt": nrm(ks[0], (BATCH, SEQ, D_MODEL), 1.0),
        "x_sample": nrm(ks[1], (DEC_BATCH, DEC_SEQ, D_MODEL), 1.0),
        "c_prompt": nrm(ks[2], (BATCH, D_MODEL), 1.0),
        "c_sample": nrm(ks[3], (DEC_BATCH, D_MODEL), 1.0),
        "ada_w": nrm(ks[4], (DEPTH, D_MODEL, N_MOD * D_MODEL), 0.5 * D_MODEL ** -0.5),
        "ada_b": nrm(ks[5], (DEPTH, N_MOD * D_MODEL), 0.02),
        "norm_ffn1": gain(ks[6], (DEPTH, D_MODEL)),
        "norm_mix": gain(ks[7], (DEPTH, D_MODEL)),
        "norm_ffn2": gain(ks[8], (DEPTH, D_MODEL)),
        "ffn1_wi": nrm(ks[9], (DEPTH, D_MODEL, 2 * D_FF), D_MODEL ** -0.5),
        "ffn1_wo": nrm(ks[10], (DEPTH, D_FF, D_MODEL), D_FF ** -0.5),
        "ffn2_wi": nrm(ks[11], (DEPTH, D_MODEL, 2 * D_FF), D_MODEL ** -0.5),
        "ffn2_wo": nrm(ks[12], (DEPTH, D_FF, D_MODEL), D_FF ** -0.5),
        "w_in": nrm(ks[13], (DEPTH, D_MODEL, IN_COLS), D_MODEL ** -0.5),
        "w_out": nrm(ks[14], (DEPTH, MIX_WIDTH, D_MODEL), MIX_WIDTH ** -0.5),
        "q_norm": gain(ks[15], (DEPTH, HEAD_DIM)),
        "k_norm": gain(ks[16], (DEPTH, HEAD_DIM)),
        "lambda_qk": nrm(ks[17], (DEPTH, 4, HEAD_DIM), 0.1),
        "subln": gain(ks[18], (DEPTH, 2 * HEAD_DIM)),
        "rel_bias": nrm(ks[19], (NUM_BUCKETS, DIFF_HEADS, 2), 0.5),
    }


def reference(x_prompt, x_sample, c_prompt, c_sample, ada_w, ada_b, norm_ffn1, norm_mix,
              norm_ffn2, ffn1_wi, ffn1_wo, ffn2_wi, ffn2_wo, w_in, w_out, q_norm, k_norm,
              lambda_qk, subln, rel_bias):
    y_prompt = trunk(x_prompt, c_prompt, ada_w, ada_b, norm_ffn1, norm_mix, norm_ffn2,
                     ffn1_wi, ffn1_wo, ffn2_wi, ffn2_wo, w_in, w_out, q_norm, k_norm,
                     lambda_qk, subln, rel_bias)
    y_sample = trunk(x_sample, c_sample, ada_w, ada_b, norm_ffn1, norm_mix, norm_ffn2,
                     ffn1_wi, ffn1_wo, ffn2_wi, ffn2_wo, w_in, w_out, q_norm, k_norm,
                     lambda_qk, subln, rel_bias)
    return (y_prompt, y_sample)
```

```python
import functools
import math

import numpy as np
import jax
import jax.numpy as jnp
from jax import lax
from jax.experimental import pallas as pl
from jax.experimental.pallas import tpu as pltpu

D_MODEL = 1024
HEAD_DIM = 64
N_HEADS = 4
HEAD_W = 2 * HEAD_DIM
QK_W = N_HEADS * HEAD_W
V_W = QK_W
F_W = 512
F_GROUP = 64
D_FF = 2816
N_MOD = 9
NUM_BUCKETS = 32
REL_MAX_DISTANCE = 128
EPS = 1e-6
ATTN_SCALE = HEAD_DIM ** -0.5

V7X_VMEM_LIMIT_BYTES = 56 * 1024 * 1024
MOD_ROWS = 16
MOD_TN = 1024
FFN_TM = 512
FFN_CK = 256
MIX_TM = 512
ATT_T = 256
N_BIAS_TILES = 5
OUT_TM = 256
DFT_ROWS = 64

f32 = jnp.float32
bf16 = jnp.bfloat16


def _dot(a, b):
    return jnp.dot(a, b, preferred_element_type=f32)


def _params(sem, vmem=V7X_VMEM_LIMIT_BYTES):
    return pltpu.CompilerParams(dimension_semantics=sem, vmem_limit_bytes=vmem)


def _resident(shape):
    return pl.BlockSpec(shape, lambda *_: (0,) * len(shape), pipeline_mode=pl.Buffered(1))


def _split_bf16(a):
    hi = a.astype(bf16)
    lo = (a - hi.astype(f32)).astype(bf16)
    return hi, lo


def _mod_norm(x, gain, shift, scale):
    ms = jnp.mean(x * x, axis=-1, keepdims=True)
    return (x * lax.rsqrt(ms + EPS) * gain) * (1.0 + scale) + shift


def _mod_kernel(c_ref, w_ref, b_ref, o_ref):
    c = c_ref[...]
    sc = c * (1.0 / (1.0 + jnp.exp(-c)))
    s_hi, s_lo = _split_bf16(sc)
    w_hi, w_lo = _split_bf16(w_ref[0])
    o_ref[0] = _dot(s_hi, w_hi) + _dot(s_hi, w_lo) + _dot(s_lo, w_hi) + b_ref[0]


def _mod_call(c_all, ada_w, ada_b):
    depth, _, n = ada_w.shape
    return pl.pallas_call(
        _mod_kernel,
        grid=(depth, n // MOD_TN),
        in_specs=[
            pl.BlockSpec((MOD_ROWS, D_MODEL), lambda l, j: (0, 0)),
            pl.BlockSpec((1, D_MODEL, MOD_TN), lambda l, j: (l, 0, j)),
            pl.BlockSpec((1, 1, MOD_TN), lambda l, j: (l, 0, j)),
        ],
        out_specs=pl.BlockSpec((1, MOD_ROWS, MOD_TN), lambda l, j: (l, 0, j)),
        out_shape=jax.ShapeDtypeStruct((depth, MOD_ROWS, n), f32),
        compiler_params=_params(("arbitrary", "arbitrary")),
        name="adaln_mod",
    )(c_all, ada_w, ada_b.reshape(depth, 1, n))


def _ffn_kernel(x_ref, mod_ref, g_ref, wi_ref, wo_ref, o_ref, acc_ref, *, row0):
    x = x_ref[0]
    shift, scale, gate = (mod_ref[0, row0 + i:row0 + i + 1, :] for i in range(3))
    hb = _mod_norm(x, g_ref[...], shift, scale).astype(bf16)
    for c in range(D_FF // FFN_CK):
        g = _dot(hb, wi_ref[:, c * FFN_CK:(c + 1) * FFN_CK])
        u = _dot(hb, wi_ref[:, D_FF + c * FFN_CK:D_FF + (c + 1) * FFN_CK])
        a = (g * (1.0 / (1.0 + jnp.exp(-g))) * u).astype(bf16)
        part = _dot(a, wo_ref[c * FFN_CK:(c + 1) * FFN_CK, :])
        if c == 0:
            acc_ref[...] = part
        else:
            acc_ref[...] += part
    o_ref[0] = x + (0.5 * gate) * acc_ref[...]


def _ffn_call(x, mod, gain, wi, wo, row0):
    b, s, d = x.shape
    tm = min(FFN_TM, s)
    return pl.pallas_call(
        functools.partial(_ffn_kernel, row0=row0),
        grid=(b, s // tm),
        in_specs=[
            pl.BlockSpec((1, tm, d), lambda i, j: (i, j, 0)),
            pl.BlockSpec((1, N_MOD, d), lambda i, j: (i, 0, 0)),
            _resident((1, d)),
            _resident((d, 2 * D_FF)),
            _resident((D_FF, d)),
        ],
        out_specs=pl.BlockSpec((1, tm, d), lambda i, j: (i, j, 0)),
        out_shape=jax.ShapeDtypeStruct(x.shape, f32),
        scratch_shapes=[pltpu.VMEM((tm, d), f32)],
        compiler_params=_params(("parallel", "parallel")),
        name="ffn",
    )(x, mod, gain, wi, wo)


def _group_rms(z, gmean_ref, gain):
    outs = []
    for c in range(QK_W // 256):
        zc = z[:, c * 256:(c + 1) * 256]
        hi, lo = _split_bf16(zc * zc)
        ms = _dot(hi, gmean_ref[...]) + _dot(lo, gmean_ref[...])
        outs.append(zc * lax.rsqrt(ms + EPS))
    return jnp.concatenate(outs, axis=-1) * gain


def _mixin_kernel(x_ref, mod_ref, g_ref, win_ref, gq_ref, gk_ref, gmean_ref, dc_ref, ds_ref,
                  q_ref, kt_ref, v_ref, xc_ref, xs_ref):
    x = x_ref[0]
    shift, scale = mod_ref[0, 3:4, :], mod_ref[0, 4:5, :]
    hb = _mod_norm(x, g_ref[...], shift, scale).astype(bf16)
    zq = _dot(hb, win_ref[:, 0:QK_W])
    q_ref[0] = (_group_rms(zq, gmean_ref, gq_ref[...]) * ATTN_SCALE).astype(bf16)
    zk = _dot(hb, win_ref[:, QK_W:2 * QK_W])
    kt_ref[0] = _group_rms(zk, gmean_ref, gk_ref[...]).T.astype(bf16)
    v_ref[0] = _dot(hb, win_ref[:, 2 * QK_W:2 * QK_W + V_W]).astype(bf16)
    fb = _dot(hb, win_ref[:, 2 * QK_W + V_W:]).astype(bf16)
    for c in range(F_W // 256):
        fc = fb[:, c * 256:(c + 1) * 256]
        xc_ref[0, :, c * 256:(c + 1) * 256] = _dot(fc, dc_ref[...]).astype(bf16)
        xs_ref[0, :, c * 256:(c + 1) * 256] = _dot(fc, ds_ref[...]).astype(bf16)


def _mixin_call(x, mod, gain, w_in, gq, gk, gmean, dc, ds):
    b, s, d = x.shape
    tm = min(MIX_TM, s)
    row = lambda i, j: (i, j, 0)
    tok = lambda w: jax.ShapeDtypeStruct((b, s, w), bf16)
    return pl.pallas_call(
        _mixin_kernel,
        grid=(b, s // tm),
        in_specs=[
            pl.BlockSpec((1, tm, d), row),
            pl.BlockSpec((1, N_MOD, d), lambda i, j: (i, 0, 0)),
            _resident((1, d)),
            _resident(w_in.shape),
            _resident((1, QK_W)),
            _resident((1, QK_W)),
            _resident((256, 256)),
            _resident((256, 256)),
            _resident((256, 256)),
        ],
        out_specs=[
            pl.BlockSpec((1, tm, QK_W), row),
            pl.BlockSpec((1, QK_W, tm), lambda i, j: (i, 0, j)),
            pl.BlockSpec((1, tm, V_W), row),
            pl.BlockSpec((1, tm, F_W), row),
            pl.BlockSpec((1, tm, F_W), row),
        ],
        out_shape=[tok(QK_W), jax.ShapeDtypeStruct((b, QK_W, s), bf16), tok(V_W), tok(F_W), tok(F_W)],
        compiler_params=_params(("parallel", "parallel")),
        name="mix_in",
    )(x, mod, gain, w_in, gq, gk, gmean, dc, ds)


def _attn_kernel(lqk_ref, q_ref, kt_ref, v_ref, bias_ref, g_ref, o_ref, s_scr, *, nk, lambda_init):
    t = ATT_T
    qt = pl.program_id(2)
    q = q_ref[0]
    lane = lax.broadcasted_iota(jnp.int32, q.shape, 1)
    zero = jnp.zeros_like(q)
    qs = jnp.concatenate([jnp.where(lane < HEAD_DIM, q, zero), jnp.where(lane >= HEAD_DIM, q, zero)], axis=0)

    m_acc = [jnp.full((t, 128), -jnp.inf, f32) for _ in range(2)]
    for kt in range(nk):
        s = _dot(qs, kt_ref[0, :, kt * t:(kt + 1) * t])
        tile = jnp.clip(kt - qt + 2, 0, N_BIAS_TILES - 1)
        for j in range(2):
            sj = s[j * t:(j + 1) * t] + bias_ref[0, j, tile]
            s_scr[j, :, kt * t:(kt + 1) * t] = sj
            m_acc[j] = jnp.maximum(m_acc[j], jnp.maximum(sj[:, :128], sj[:, 128:]))

    heads = []
    for j in range(2):
        m = jnp.max(m_acc[j], axis=-1, keepdims=True)
        l_acc = jnp.zeros((t, 128), f32)
        o = jnp.zeros((t, HEAD_W), f32)
        for kt in range(nk):
            p = jnp.exp(s_scr[j, :, kt * t:(kt + 1) * t] - m)
            l_acc = l_acc + (p[:, :128] + p[:, 128:])
            o = o + _dot(p.astype(bf16), v_ref[0, kt * t:(kt + 1) * t, :])
        heads.append(o / jnp.sum(l_acc, axis=-1, keepdims=True))

    lqk = lqk_ref[...]
    lam = (jnp.exp(jnp.sum(lqk[0:1] * lqk[1:2], axis=-1, keepdims=True))
           - jnp.exp(jnp.sum(lqk[2:3] * lqk[3:4], axis=-1, keepdims=True)) + lambda_init)
    o = heads[0] - lam * heads[1]
    ms = jnp.mean(o * o, axis=-1, keepdims=True)
    o_ref[0] = ((o * lax.rsqrt(ms + EPS) * g_ref[...]) * (1.0 - lambda_init)).astype(bf16)


def _attn_call(lqk, q, kt, v, bias, subln, lambda_init):
    b, s, _ = q.shape
    t = ATT_T
    return pl.pallas_call(
        functools.partial(_attn_kernel, nk=s // t, lambda_init=lambda_init),
        grid=(b, N_HEADS, s // t),
        in_specs=[
            pl.BlockSpec((4, HEAD_DIM), lambda i, h, j: (0, 0)),
            pl.BlockSpec((1, t, HEAD_W), lambda i, h, j: (i, j, h)),
            pl.BlockSpec((1, HEAD_W, s), lambda i, h, j: (i, h, 0)),
            pl.BlockSpec((1, s, HEAD_W), lambda i, h, j: (i, 0, h)),
            pl.BlockSpec((1, 2, N_BIAS_TILES, t, t), lambda i, h, j: (h, 0, 0, 0, 0)),
            pl.BlockSpec((1, HEAD_W), lambda i, h, j: (0, 0)),
        ],
        out_specs=pl.BlockSpec((1, t, HEAD_W), lambda i, h, j: (i, j, h)),
        out_shape=jax.ShapeDtypeStruct((b, s, V_W), bf16),
        scratch_shapes=[pltpu.VMEM((2, t, s), f32)],
        compiler_params=_params(("parallel", "parallel", "arbitrary")),
        name="diff_attn",
    )(lqk, q, kt, v, bias, subln)


def _mixout_kernel(x_ref, mod_ref, o_ref, xc_ref, xs_ref, cm_ref, sm_ref, wout_ref, y_ref, *, scale):
    fo = (_dot(cm_ref[...], xc_ref[0]) - _dot(sm_ref[...], xs_ref[0])) * scale
    mix = _dot(o_ref[0], wout_ref[0:V_W, :]) + _dot(fo.astype(bf16), wout_ref[V_W:, :])
    y_ref[0] = x_ref[0] + mod_ref[0, 5:6, :] * mix


def _mixout_call(x, mod, o, xc, xs, cm, sm, w_out):
    b, s, d = x.shape
    tm = min(OUT_TM, s)
    row = lambda i, j: (i, j, 0)
    seq = lambda i, j: (i, 0, 0)
    return pl.pallas_call(
        functools.partial(_mixout_kernel, scale=1.0 / math.sqrt(F_GROUP * s)),
        grid=(b, s // tm),
        in_specs=[
            pl.BlockSpec((1, tm, d), row),
            pl.BlockSpec((1, N_MOD, d), seq),
            pl.BlockSpec((1, tm, V_W), row),
            pl.BlockSpec((1, s, F_W), seq),
            pl.BlockSpec((1, s, F_W), seq),
            pl.BlockSpec((tm, s), lambda i, j: (j, 0)),
            pl.BlockSpec((tm, s), lambda i, j: (j, 0)),
            _resident(w_out.shape),
        ],
        out_specs=pl.BlockSpec((1, tm, d), row),
        out_shape=jax.ShapeDtypeStruct(x.shape, f32),
        compiler_params=_params(("parallel", "parallel")),
        name="mix_out",
    )(x, mod, o, xc, xs, cm, sm, w_out)


def _dft_kernel(ac_ref, as_ref, bc_ref, bs_ref, cm_ref, sm_ref):
    ac, asn = ac_ref[0], as_ref[0]
    bc, bsn = bc_ref[...], bs_ref[...]
    cm_ref[...] = (ac * bc - asn * bsn).astype(bf16)
    sm_ref[...] = (asn * bc + ac * bsn).astype(bf16)


def _dft_tables(s):
    pos = np.arange(s, dtype=np.int64)
    ang = lambda rows: 2.0 * np.pi * ((rows[:, None] * pos[None, :]) % s).astype(np.float64) / s
    a = ang(np.arange(s // DFT_ROWS, dtype=np.int64) * DFT_ROWS)
    b = ang(np.arange(DFT_ROWS, dtype=np.int64))
    t3 = lambda m: jnp.asarray(m.astype(np.float32)).reshape(s // DFT_ROWS, 1, s)
    return t3(np.cos(a)), t3(np.sin(a)), jnp.asarray(np.cos(b).astype(np.float32)), jnp.asarray(np.sin(b).astype(np.float32))


def _dft_call(s):
    ac, asn, bc, bsn = _dft_tables(s)
    arow = pl.BlockSpec((1, 1, s), lambda a: (a, 0, 0))
    out = pl.BlockSpec((DFT_ROWS, s), lambda a: (a, 0))
    return pl.pallas_call(
        _dft_kernel,
        grid=(s // DFT_ROWS,),
        in_specs=[arow, arow, _resident((DFT_ROWS, s)), _resident((DFT_ROWS, s))],
        out_specs=[out, out],
        out_shape=[jax.ShapeDtypeStruct((s, s), bf16)] * 2,
        compiler_params=_params(("parallel",)),
        name="dft_matrix",
    )(ac, asn, bc, bsn)


def _rel_bucket_np(rel):
    nb = NUM_BUCKETS // 2
    max_exact = nb // 2
    ret = (rel > 0).astype(np.int32) * nb
    n = np.abs(rel)
    nf = np.maximum(n, 1).astype(np.float32)
    ratio = np.log(nf / np.float32(max_exact)) / np.float32(math.log(REL_MAX_DISTANCE / max_exact))
    large = max_exact + (ratio * np.float32(nb - max_exact)).astype(np.int32)
    large = np.minimum(large, nb - 1)
    return (ret + np.where(n < max_exact, n, large)).astype(np.int32)


def _bucket_tiles():
    t = ATT_T
    qpos = np.arange(t)[:, None]
    kpos = np.arange(t)[None, :]
    return np.stack([_rel_bucket_np((d * t + kpos) - qpos) for d in range(-2, 3)])


def _bias_kernel(table_ref, bucket_ref, o_ref):
    hj = pl.program_id(0)
    bucket = bucket_ref[...]
    acc = jnp.zeros(bucket.shape, f32)
    for bkt in range(NUM_BUCKETS):
        acc = jnp.where(bucket == bkt, table_ref[bkt, hj], acc)
    o_ref[0] = acc


def _bias_call(rel_bias):
    t = ATT_T
    table = rel_bias.reshape(NUM_BUCKETS, 2 * N_HEADS)
    tiles = pl.pallas_call(
        _bias_kernel,
        grid=(2 * N_HEADS,),
        in_specs=[
            pl.BlockSpec(memory_space=pltpu.SMEM),
            _resident((N_BIAS_TILES, t, t)),
        ],
        out_specs=pl.BlockSpec((1, N_BIAS_TILES, t, t), lambda i: (i, 0, 0, 0)),
        out_shape=jax.ShapeDtypeStruct((2 * N_HEADS, N_BIAS_TILES, t, t), f32),
        compiler_params=_params(("arbitrary",)),
        name="rel_bias_tiles",
    )(table, jnp.asarray(_bucket_tiles()))
    return tiles.reshape(N_HEADS, 2, N_BIAS_TILES, t, t)


def _channel_dft_tables():
    idx = np.arange(F_GROUP)
    ang = 2.0 * np.pi * ((idx[:, None] * idx[None, :]) % F_GROUP) / F_GROUP
    eye = np.eye(256 // F_GROUP)
    return (jnp.asarray(np.kron(eye, np.cos(ang)), dtype=bf16), jnp.asarray(np.kron(eye, np.sin(ang)), dtype=bf16))


def _group_mean_table():
    return jnp.asarray(np.kron(np.eye(256 // HEAD_DIM), np.full((HEAD_DIM, HEAD_DIM), 1.0 / HEAD_DIM)), dtype=bf16)


def _trunk(x, mods, p, bias, cm, sm, consts):
    gmean, dc, ds = consts
    depth = p["w_in"].shape[0]
    for l in range(depth):
        lambda_init = 0.8 - 0.6 * math.exp(-0.3 * l)
        mod = mods[l]
        x = _ffn_call(x, mod, p["norm_ffn1"][l], p["ffn1_wi"][l], p["ffn1_wo"][l], 0)
        q, kt, v, xc, xs = _mixin_call(x, mod, p["norm_mix"][l], p["w_in"][l], p["q_norm"][l], p["k_norm"][l],
                                       gmean, dc, ds)
        o = _attn_call(p["lambda_qk"][l], q, kt, v, bias, p["subln"][l], lambda_init)
        x = _mixout_call(x, mod, o, xc, xs, cm, sm, p["w_out"][l])
        x = _ffn_call(x, mod, p["norm_ffn2"][l], p["ffn2_wi"][l], p["ffn2_wo"][l], 6)
    return x


def kernel(x_prompt, x_sample, c_prompt, c_sample, ada_w, ada_b, norm_ffn1, norm_mix, norm_ffn2, ffn1_wi, ffn1_wo,
           ffn2_wi, ffn2_wo, w_in, w_out, q_norm, k_norm, lambda_qk, subln, rel_bias):
    depth = ada_w.shape[0]
    nb_p, nb_s = c_prompt.shape[0], c_sample.shape[0]
    assert nb_p + nb_s <= MOD_ROWS
    c_all = jnp.concatenate([c_prompt, c_sample, jnp.zeros((MOD_ROWS - nb_p - nb_s, D_MODEL), f32)], axis=0)
    mod_all = _mod_call(c_all, ada_w, ada_b).reshape(depth, MOD_ROWS, N_MOD, D_MODEL)

    row = lambda a: a.reshape(depth, 1, -1)
    tile_heads = lambda a: jnp.tile(a, (1, QK_W // HEAD_DIM)).reshape(depth, 1, QK_W)
    p = dict(
        norm_ffn1=row(norm_ffn1), norm_mix=row(norm_mix), norm_ffn2=row(norm_ffn2),
        ffn1_wi=ffn1_wi.astype(bf16), ffn1_wo=ffn1_wo.astype(bf16),
        ffn2_wi=ffn2_wi.astype(bf16), ffn2_wo=ffn2_wo.astype(bf16),
        w_in=w_in.astype(bf16), w_out=w_out.astype(bf16),
        q_norm=tile_heads(q_norm), k_norm=tile_heads(k_norm),
        lambda_qk=lambda_qk, subln=row(subln),
    )
    bias = _bias_call(rel_bias)
    consts = (_group_mean_table(),) + _channel_dft_tables()

    outs = []
    for x, lo, hi in ((x_prompt, 0, nb_p), (x_sample, nb_p, nb_p + nb_s)):
        cm, sm = _dft_call(x.shape[1])
        mods = [mod_all[l, lo:hi] for l in range(depth)]
        outs.append(_trunk(x, mods, p, bias, cm, sm, consts))
    return tuple(outs)
```

```python
import functools
import math

import numpy as np
import jax
import jax.numpy as jnp
from jax import lax
from jax.experimental import pallas as pl
from jax.experimental.pallas import tpu as pltpu

D_MODEL = 1024
HEAD_DIM = 64
N_HEADS = 4
HEAD_W = 2 * HEAD_DIM
QK_W = N_HEADS * HEAD_W
V_W = QK_W
F_W = 512
F_GROUP = 64
D_FF = 2816
N_MOD = 9
NUM_BUCKETS = 32
REL_MAX_DISTANCE = 128
EPS = 1e-6
ATTN_SCALE = HEAD_DIM ** -0.5

V7X_VMEM_LIMIT_BYTES = 56 * 1024 * 1024
MOD_ROWS = 16
MOD_TN = 1024
FFN_TM = 512
FFN_CK = 256
MIX_TM = 512
ATT_TQ = 512
ATT_TK = 256
ATT_UNROLL = 8
N_BIAS_TILES = ATT_TQ // ATT_TK + 4
LOG2E = math.log2(math.e)
OUT_TM = 256
DFT_ROWS = 64

f32 = jnp.float32
bf16 = jnp.bfloat16


def _dot(a, b):
    return jnp.dot(a, b, preferred_element_type=f32)


def _params(sem, vmem=V7X_VMEM_LIMIT_BYTES):
    return pltpu.CompilerParams(dimension_semantics=sem, vmem_limit_bytes=vmem)


def _resident(shape):
    return pl.BlockSpec(shape, lambda *_: (0,) * len(shape), pipeline_mode=pl.Buffered(1))


def _split_bf16(a):
    hi = a.astype(bf16)
    lo = (a - hi.astype(f32)).astype(bf16)
    return hi, lo


def _mod_norm(x, gain, shift, scale):
    ms = jnp.mean(x * x, axis=-1, keepdims=True)
    return (x * lax.rsqrt(ms + EPS) * gain) * (1.0 + scale) + shift


def _mod_kernel(c_ref, w_ref, b_ref, o_ref):
    c = c_ref[...]
    sc = c * (1.0 / (1.0 + jnp.exp(-c)))
    s_hi, s_lo = _split_bf16(sc)
    w_hi, w_lo = _split_bf16(w_ref[0])
    o_ref[0] = _dot(s_hi, w_hi) + _dot(s_hi, w_lo) + _dot(s_lo, w_hi) + b_ref[0]


def _mod_call(c_all, ada_w, ada_b):
    depth, _, n = ada_w.shape
    return pl.pallas_call(
        _mod_kernel,
        grid=(depth, n // MOD_TN),
        in_specs=[
            pl.BlockSpec((MOD_ROWS, D_MODEL), lambda l, j: (0, 0)),
            pl.BlockSpec((1, D_MODEL, MOD_TN), lambda l, j: (l, 0, j)),
            pl.BlockSpec((1, 1, MOD_TN), lambda l, j: (l, 0, j)),
        ],
        out_specs=pl.BlockSpec((1, MOD_ROWS, MOD_TN), lambda l, j: (l, 0, j)),
        out_shape=jax.ShapeDtypeStruct((depth, MOD_ROWS, n), f32),
        compiler_params=_params(("arbitrary", "arbitrary")),
        name="adaln_mod",
    )(c_all, ada_w, ada_b.reshape(depth, 1, n))


def _ffn_kernel(x_ref, mod_ref, g_ref, wi_ref, wo_ref, o_ref, acc_ref, *, row0):
    x = x_ref[0]
    shift, scale, gate = (mod_ref[0, row0 + i:row0 + i + 1, :] for i in range(3))
    hb = _mod_norm(x, g_ref[...], shift, scale).astype(bf16)
    for c in range(D_FF // FFN_CK):
        g = _dot(hb, wi_ref[:, c * FFN_CK:(c + 1) * FFN_CK])
        u = _dot(hb, wi_ref[:, D_FF + c * FFN_CK:D_FF + (c + 1) * FFN_CK])
        a = (g * (1.0 / (1.0 + jnp.exp(-g))) * u).astype(bf16)
        part = _dot(a, wo_ref[c * FFN_CK:(c + 1) * FFN_CK, :])
        if c == 0:
            acc_ref[...] = part
        else:
            acc_ref[...] += part
    o_ref[0] = x + (0.5 * gate) * acc_ref[...]


def _ffn_call(x, mod, gain, wi, wo, row0):
    b, s, d = x.shape
    tm = min(FFN_TM, s)
    return pl.pallas_call(
        functools.partial(_ffn_kernel, row0=row0),
        grid=(b, s // tm),
        in_specs=[
            pl.BlockSpec((1, tm, d), lambda i, j: (i, j, 0)),
            pl.BlockSpec((1, N_MOD, d), lambda i, j: (i, 0, 0)),
            _resident((1, d)),
            _resident((d, 2 * D_FF)),
            _resident((D_FF, d)),
        ],
        out_specs=pl.BlockSpec((1, tm, d), lambda i, j: (i, j, 0)),
        out_shape=jax.ShapeDtypeStruct(x.shape, f32),
        scratch_shapes=[pltpu.VMEM((tm, d), f32)],
        compiler_params=_params(("parallel", "parallel")),
        name="ffn",
    )(x, mod, gain, wi, wo)


def _group_rms(z, gmean_ref, gain):
    outs = []
    for c in range(QK_W // 256):
        zc = z[:, c * 256:(c + 1) * 256]
        hi, lo = _split_bf16(zc * zc)
        ms = _dot(hi, gmean_ref[...]) + _dot(lo, gmean_ref[...])
        outs.append(zc * lax.rsqrt(ms + EPS))
    return jnp.concatenate(outs, axis=-1) * gain


def _mixin_kernel(x_ref, mod_ref, g_ref, win_ref, gq_ref, gk_ref, gmean_ref, dc_ref, ds_ref,
                  q_ref, kt_ref, v_ref, xc_ref, xs_ref):
    x = x_ref[0]
    shift, scale = mod_ref[0, 3:4, :], mod_ref[0, 4:5, :]
    hb = _mod_norm(x, g_ref[...], shift, scale).astype(bf16)
    zq = _dot(hb, win_ref[:, 0:QK_W])
    q_ref[0] = (_group_rms(zq, gmean_ref, gq_ref[...]) * (ATTN_SCALE * LOG2E)).astype(bf16)
    zk = _dot(hb, win_ref[:, QK_W:2 * QK_W])
    kt_ref[0] = _group_rms(zk, gmean_ref, gk_ref[...]).T.astype(bf16)
    v = _dot(hb, win_ref[:, 2 * QK_W:2 * QK_W + V_W]).astype(bf16)
    ones = jnp.ones((v.shape[0], HEAD_W), bf16)
    for h in range(N_HEADS):
        v_ref[0, :, 2 * h * HEAD_W:(2 * h + 1) * HEAD_W] = v[:, h * HEAD_W:(h + 1) * HEAD_W]
        v_ref[0, :, (2 * h + 1) * HEAD_W:(2 * h + 2) * HEAD_W] = ones
    fb = _dot(hb, win_ref[:, 2 * QK_W + V_W:]).astype(bf16)
    for c in range(F_W // 256):
        fc = fb[:, c * 256:(c + 1) * 256]
        xc_ref[0, :, c * 256:(c + 1) * 256] = _dot(fc, dc_ref[...]).astype(bf16)
        xs_ref[0, :, c * 256:(c + 1) * 256] = _dot(fc, ds_ref[...]).astype(bf16)


def _mixin_call(x, mod, gain, w_in, gq, gk, gmean, dc, ds):
    b, s, d = x.shape
    tm = min(MIX_TM, s)
    row = lambda i, j: (i, j, 0)
    tok = lambda w: jax.ShapeDtypeStruct((b, s, w), bf16)
    return pl.pallas_call(
        _mixin_kernel,
        grid=(b, s // tm),
        in_specs=[
            pl.BlockSpec((1, tm, d), row),
            pl.BlockSpec((1, N_MOD, d), lambda i, j: (i, 0, 0)),
            _resident((1, d)),
            _resident(w_in.shape),
            _resident((1, QK_W)),
            _resident((1, QK_W)),
            _resident((256, 256)),
            _resident((256, 256)),
            _resident((256, 256)),
        ],
        out_specs=[
            pl.BlockSpec((1, tm, QK_W), row),
            pl.BlockSpec((1, QK_W, tm), lambda i, j: (i, 0, j)),
            pl.BlockSpec((1, tm, 2 * V_W), row),
            pl.BlockSpec((1, tm, F_W), row),
            pl.BlockSpec((1, tm, F_W), row),
        ],
        out_shape=[tok(QK_W), jax.ShapeDtypeStruct((b, QK_W, s), bf16), tok(2 * V_W), tok(F_W), tok(F_W)],
        compiler_params=_params(("parallel", "parallel")),
        name="mix_in",
    )(x, mod, gain, w_in, gq, gk, gmean, dc, ds)


def _attn_kernel(lqk_ref, q_ref, kt_ref, v_ref, bias_ref, g_ref, o_ref, s_scr, m_scr, *, n_chunks, unroll,
                 lambda_init):
    tq, tk = ATT_TQ, ATT_TK
    qt = pl.program_id(2)
    q = q_ref[0]
    lane = lax.broadcasted_iota(jnp.int32, q.shape, 1)
    zero = jnp.zeros_like(q)
    qs = jnp.concatenate([jnp.where(lane < HEAD_DIM, q, zero), jnp.where(lane >= HEAD_DIM, q, zero)], axis=0)

    m_scr[...] = jnp.full(m_scr.shape, -jnp.inf, f32)

    def score_chunk(c, carry):
        for u in range(unroll):
            kt = c * unroll + u
            k0 = pl.multiple_of(kt * tk, tk)
            s = _dot(qs, kt_ref[0, :, pl.ds(k0, tk)])
            tile = jnp.clip(kt - qt * (tq // tk) + 2, 0, N_BIAS_TILES - 1)
            for j in range(2):
                sj = s[j * tq:(j + 1) * tq] + bias_ref[0, j, tile]
                s_scr[j, :, pl.ds(k0, tk)] = sj
                m_scr[j] = jnp.maximum(m_scr[j], jnp.maximum(sj[:, :128], sj[:, 128:]))
        return carry

    lax.fori_loop(0, n_chunks, score_chunk, 0)

    heads = []
    for j in range(2):
        m = jnp.max(m_scr[j], axis=-1, keepdims=True)
        p = jnp.exp2(s_scr[j] - m).astype(bf16)
        ov = _dot(p, v_ref[0])
        heads.append(ov[:, :HEAD_W] / ov[:, HEAD_W:])

    lqk = lqk_ref[...]
    lam = (jnp.exp(jnp.sum(lqk[0:1] * lqk[1:2], axis=-1, keepdims=True))
           - jnp.exp(jnp.sum(lqk[2:3] * lqk[3:4], axis=-1, keepdims=True)) + lambda_init)
    o = heads[0] - lam * heads[1]
    ms = jnp.mean(o * o, axis=-1, keepdims=True)
    o_ref[0] = ((o * lax.rsqrt(ms + EPS) * g_ref[...]) * (1.0 - lambda_init)).astype(bf16)


def _attn_call(lqk, q, kt, vx, bias, subln, lambda_init):
    b, s, _ = q.shape
    tq, tk = ATT_TQ, ATT_TK
    unroll = min(ATT_UNROLL, max(1, s // (2 * tk)))
    assert s % (tk * unroll) == 0 and s % tq == 0
    return pl.pallas_call(
        functools.partial(_attn_kernel, n_chunks=s // (tk * unroll), unroll=unroll, lambda_init=lambda_init),
        grid=(b, N_HEADS, s // tq),
        in_specs=[
            pl.BlockSpec((4, HEAD_DIM), lambda i, h, j: (0, 0)),
            pl.BlockSpec((1, tq, HEAD_W), lambda i, h, j: (i, j, h)),
            pl.BlockSpec((1, HEAD_W, s), lambda i, h, j: (i, h, 0)),
            pl.BlockSpec((1, s, 2 * HEAD_W), lambda i, h, j: (i, 0, h)),
            pl.BlockSpec((1, 2, N_BIAS_TILES, tq, tk), lambda i, h, j: (h, 0, 0, 0, 0)),
            pl.BlockSpec((1, HEAD_W), lambda i, h, j: (0, 0)),
        ],
        out_specs=pl.BlockSpec((1, tq, HEAD_W), lambda i, h, j: (i, j, h)),
        out_shape=jax.ShapeDtypeStruct((b, s, V_W), bf16),
        scratch_shapes=[pltpu.VMEM((2, tq, s), f32), pltpu.VMEM((2, tq, 128), f32)],
        compiler_params=_params(("parallel", "parallel", "arbitrary")),
        name="diff_attn",
    )(lqk, q, kt, vx, bias, subln)


def _mixout_kernel(x_ref, mod_ref, o_ref, xc_ref, xs_ref, cm_ref, sm_ref, wout_ref, y_ref, *, scale):
    fo = (_dot(cm_ref[...], xc_ref[0]) - _dot(sm_ref[...], xs_ref[0])) * scale
    mix = _dot(o_ref[0], wout_ref[0:V_W, :]) + _dot(fo.astype(bf16), wout_ref[V_W:, :])
    y_ref[0] = x_ref[0] + mod_ref[0, 5:6, :] * mix


def _mixout_call(x, mod, o, xc, xs, cm, sm, w_out):
    b, s, d = x.shape
    tm = min(OUT_TM, s)
    row = lambda i, j: (i, j, 0)
    seq = lambda i, j: (i, 0, 0)
    return pl.pallas_call(
        functools.partial(_mixout_kernel, scale=1.0 / math.sqrt(F_GROUP * s)),
        grid=(b, s // tm),
        in_specs=[
            pl.BlockSpec((1, tm, d), row),
            pl.BlockSpec((1, N_MOD, d), seq),
            pl.BlockSpec((1, tm, V_W), row),
            pl.BlockSpec((1, s, F_W), seq),
            pl.BlockSpec((1, s, F_W), seq),
            pl.BlockSpec((tm, s), lambda i, j: (j, 0)),
            pl.BlockSpec((tm, s), lambda i, j: (j, 0)),
            _resident(w_out.shape),
        ],
        out_specs=pl.BlockSpec((1, tm, d), row),
        out_shape=jax.ShapeDtypeStruct(x.shape, f32),
        compiler_params=_params(("parallel", "parallel")),
        name="mix_out",
    )(x, mod, o, xc, xs, cm, sm, w_out)


def _dft_kernel(ac_ref, as_ref, bc_ref, bs_ref, cm_ref, sm_ref):
    ac, asn = ac_ref[0], as_ref[0]
    bc, bsn = bc_ref[...], bs_ref[...]
    cm_ref[...] = (ac * bc - asn * bsn).astype(bf16)
    sm_ref[...] = (asn * bc + ac * bsn).astype(bf16)


def _dft_tables(s):
    pos = np.arange(s, dtype=np.int64)
    ang = lambda rows: 2.0 * np.pi * ((rows[:, None] * pos[None, :]) % s).astype(np.float64) / s
    a = ang(np.arange(s // DFT_ROWS, dtype=np.int64) * DFT_ROWS)
    b = ang(np.arange(DFT_ROWS, dtype=np.int64))
    t3 = lambda m: jnp.asarray(m.astype(np.float32)).reshape(s // DFT_ROWS, 1, s)
    return t3(np.cos(a)), t3(np.sin(a)), jnp.asarray(np.cos(b).astype(np.float32)), jnp.asarray(np.sin(b).astype(np.float32))


def _dft_call(s):
    ac, asn, bc, bsn = _dft_tables(s)
    arow = pl.BlockSpec((1, 1, s), lambda a: (a, 0, 0))
    out = pl.BlockSpec((DFT_ROWS, s), lambda a: (a, 0))
    return pl.pallas_call(
        _dft_kernel,
        grid=(s // DFT_ROWS,),
        in_specs=[arow, arow, _resident((DFT_ROWS, s)), _resident((DFT_ROWS, s))],
        out_specs=[out, out],
        out_shape=[jax.ShapeDtypeStruct((s, s), bf16)] * 2,
        compiler_params=_params(("parallel",)),
        name="dft_matrix",
    )(ac, asn, bc, bsn)


def _rel_bucket_np(rel):
    nb = NUM_BUCKETS // 2
    max_exact = nb // 2
    ret = (rel > 0).astype(np.int32) * nb
    n = np.abs(rel)
    nf = np.maximum(n, 1).astype(np.float32)
    ratio = np.log(nf / np.float32(max_exact)) / np.float32(math.log(REL_MAX_DISTANCE / max_exact))
    large = max_exact + (ratio * np.float32(nb - max_exact)).astype(np.int32)
    large = np.minimum(large, nb - 1)
    return (ret + np.where(n < max_exact, n, large)).astype(np.int32)


def _bucket_tiles():
    qpos = np.arange(ATT_TQ)[:, None]
    kpos = np.arange(ATT_TK)[None, :]
    return np.stack([_rel_bucket_np((d * ATT_TK + kpos) - qpos) for d in range(-2, N_BIAS_TILES - 2)])


def _bias_kernel(table_ref, bucket_ref, o_ref):
    hj = pl.program_id(0)
    bucket = bucket_ref[...]
    acc = jnp.zeros(bucket.shape, f32)
    for bkt in range(NUM_BUCKETS):
        acc = jnp.where(bucket == bkt, table_ref[bkt, hj], acc)
    o_ref[0] = acc * LOG2E


def _bias_call(rel_bias):
    table = rel_bias.reshape(NUM_BUCKETS, 2 * N_HEADS)
    tiles = pl.pallas_call(
        _bias_kernel,
        grid=(2 * N_HEADS,),
        in_specs=[
            pl.BlockSpec(memory_space=pltpu.SMEM),
            _resident((N_BIAS_TILES, ATT_TQ, ATT_TK)),
        ],
        out_specs=pl.BlockSpec((1, N_BIAS_TILES, ATT_TQ, ATT_TK), lambda i: (i, 0, 0, 0)),
        out_shape=jax.ShapeDtypeStruct((2 * N_HEADS, N_BIAS_TILES, ATT_TQ, ATT_TK), f32),
        compiler_params=_params(("arbitrary",)),
        name="rel_bias_tiles",
    )(table, jnp.asarray(_bucket_tiles()))
    return tiles.reshape(N_HEADS, 2, N_BIAS_TILES, ATT_TQ, ATT_TK)


def _channel_dft_tables():
    idx = np.arange(F_GROUP)
    ang = 2.0 * np.pi * ((idx[:, None] * idx[None, :]) % F_GROUP) / F_GROUP
    eye = np.eye(256 // F_GROUP)
    return (jnp.asarray(np.kron(eye, np.cos(ang)), dtype=bf16), jnp.asarray(np.kron(eye, np.sin(ang)), dtype=bf16))


def _group_mean_table():
    return jnp.asarray(np.kron(np.eye(256 // HEAD_DIM), np.full((HEAD_DIM, HEAD_DIM), 1.0 / HEAD_DIM)), dtype=bf16)


def _trunk(x, mods, p, bias, cm, sm, consts):
    gmean, dc, ds = consts
    depth = p["w_in"].shape[0]
    for l in range(depth):
        lambda_init = 0.8 - 0.6 * math.exp(-0.3 * l)
        mod = mods[l]
        x = _ffn_call(x, mod, p["norm_ffn1"][l], p["ffn1_wi"][l], p["ffn1_wo"][l], 0)
        q, kt, v, xc, xs = _mixin_call(x, mod, p["norm_mix"][l], p["w_in"][l], p["q_norm"][l], p["k_norm"][l],
                                       gmean, dc, ds)
        o = _attn_call(p["lambda_qk"][l], q, kt, v, bias, p["subln"][l], lambda_init)
        x = _mixout_call(x, mod, o, xc, xs, cm, sm, p["w_out"][l])
        x = _ffn_call(x, mod, p["norm_ffn2"][l], p["ffn2_wi"][l], p["ffn2_wo"][l], 6)
    return x


def kernel(x_prompt, x_sample, c_prompt, c_sample, ada_w, ada_b, norm_ffn1, norm_mix, norm_ffn2, ffn1_wi, ffn1_wo,
           ffn2_wi, ffn2_wo, w_in, w_out, q_norm, k_norm, lambda_qk, subln, rel_bias):
    depth = ada_w.shape[0]
    nb_p, nb_s = c_prompt.shape[0], c_sample.shape[0]
    assert nb_p + nb_s <= MOD_ROWS
    c_all = jnp.concatenate([c_prompt, c_sample, jnp.zeros((MOD_ROWS - nb_p - nb_s, D_MODEL), f32)], axis=0)
    mod_all = _mod_call(c_all, ada_w, ada_b).reshape(depth, MOD_ROWS, N_MOD, D_MODEL)

    row = lambda a: a.reshape(depth, 1, -1)
    tile_heads = lambda a: jnp.tile(a, (1, QK_W // HEAD_DIM)).reshape(depth, 1, QK_W)
    p = dict(
        norm_ffn1=row(norm_ffn1), norm_mix=row(norm_mix), norm_ffn2=row(norm_ffn2),
        ffn1_wi=ffn1_wi.astype(bf16), ffn1_wo=ffn1_wo.astype(bf16),
        ffn2_wi=ffn2_wi.astype(bf16), ffn2_wo=ffn2_wo.astype(bf16),
        w_in=w_in.astype(bf16), w_out=w_out.astype(bf16),
        q_norm=tile_heads(q_norm), k_norm=tile_heads(k_norm),
        lambda_qk=lambda_qk, subln=row(subln),
    )
    bias = _bias_call(rel_bias)
    consts = (_group_mean_table(),) + _channel_dft_tables()

    outs = []
    for x, lo, hi in ((x_prompt, 0, nb_p), (x_sample, nb_p, nb_p + nb_s)):
        cm, sm = _dft_call(x.shape[1])
        mods = [mod_all[l, lo:hi] for l in range(depth)]
        outs.append(_trunk(x, mods, p, bias, cm, sm, consts))
    return tuple(outs)
```

```python
import functools
import math

import numpy as np
import jax
import jax.numpy as jnp
from jax import lax
from jax.experimental import pallas as pl
from jax.experimental.pallas import tpu as pltpu

D_MODEL = 1024
HEAD_DIM = 64
N_HEADS = 4
HEAD_W = 2 * HEAD_DIM
QK_W = N_HEADS * HEAD_W
V_W = QK_W
F_W = 512
F_GROUP = 64
D_FF = 2816
N_MOD = 9
NUM_BUCKETS = 32
REL_MAX_DISTANCE = 128
EPS = 1e-6
ATTN_SCALE = HEAD_DIM ** -0.5

V7X_VMEM_LIMIT_BYTES = 56 * 1024 * 1024
MOD_ROWS = 16
MOD_TN = 1024
FFN_TM = 512
FFN_CK = 256
MIX_TM = 512
ATT_TQ = 512
ATT_TK = 256
ATT_UNROLL = 8
N_BIAS_TILES = ATT_TQ // ATT_TK + 4
BIAS_ROWS = 16
LOG2E = math.log2(math.e)
OUT_TM = 256
DFT_B = 64
DFT_A_PER_STEP = 4

f32 = jnp.float32
bf16 = jnp.bfloat16


def _dot(a, b):
    return jnp.dot(a, b, preferred_element_type=f32)


def _params(sem, vmem=V7X_VMEM_LIMIT_BYTES):
    return pltpu.CompilerParams(dimension_semantics=sem, vmem_limit_bytes=vmem)


def _resident(shape):
    return pl.BlockSpec(shape, lambda *_: (0,) * len(shape), pipeline_mode=pl.Buffered(1))


def _layer(shape, l):
    return pl.BlockSpec((None,) + tuple(shape), lambda *_: (l,) + (0,) * len(shape), pipeline_mode=pl.Buffered(1))


def _split_bf16(a):
    hi = a.astype(bf16)
    lo = (a - hi.astype(f32)).astype(bf16)
    return hi, lo


def _mod_norm(x, gain, shift, scale):
    ms = jnp.mean(x * x, axis=-1, keepdims=True)
    return (x * lax.rsqrt(ms + EPS) * gain) * (1.0 + scale) + shift


def _mod_kernel(c_ref, w_ref, b_ref, o_ref):
    c = c_ref[...]
    sc = c * (1.0 / (1.0 + jnp.exp(-c)))
    s_hi, s_lo = _split_bf16(sc)
    w_hi, w_lo = _split_bf16(w_ref[0])
    o_ref[0] = _dot(s_hi, w_hi) + _dot(s_hi, w_lo) + _dot(s_lo, w_hi) + b_ref[0]


def _mod_call(c_all, ada_w, ada_b):
    depth, _, n = ada_w.shape
    return pl.pallas_call(
        _mod_kernel,
        grid=(depth, n // MOD_TN),
        in_specs=[
            pl.BlockSpec((MOD_ROWS, D_MODEL), lambda l, j: (0, 0)),
            pl.BlockSpec((1, D_MODEL, MOD_TN), lambda l, j: (l, 0, j)),
            pl.BlockSpec((1, 1, MOD_TN), lambda l, j: (l, 0, j)),
        ],
        out_specs=pl.BlockSpec((1, MOD_ROWS, MOD_TN), lambda l, j: (l, 0, j)),
        out_shape=jax.ShapeDtypeStruct((depth, MOD_ROWS, n), f32),
        compiler_params=_params(("arbitrary", "arbitrary")),
        name="adaln_mod",
    )(c_all, ada_w, ada_b.reshape(depth, 1, n))


def _ffn_kernel(x_ref, mod_ref, g_ref, wi_ref, wo_ref, o_ref, acc_ref, *, row0):
    x = x_ref[0]
    shift, scale, gate = (mod_ref[0, 0, row0 + i:row0 + i + 1, :] for i in range(3))
    hb = _mod_norm(x, g_ref[...], shift, scale).astype(bf16)
    for c in range(D_FF // FFN_CK):
        g = _dot(hb, wi_ref[:, c * FFN_CK:(c + 1) * FFN_CK])
        u = _dot(hb, wi_ref[:, D_FF + c * FFN_CK:D_FF + (c + 1) * FFN_CK])
        a = (g * (1.0 / (1.0 + jnp.exp(-g))) * u).astype(bf16)
        part = _dot(a, wo_ref[c * FFN_CK:(c + 1) * FFN_CK, :])
        if c == 0:
            acc_ref[...] = part
        else:
            acc_ref[...] += part
    o_ref[0] = x + (0.5 * gate) * acc_ref[...]


def _mod_spec(l, row0):
    return pl.BlockSpec((1, 1, N_MOD, D_MODEL), lambda i, *_: (l, row0 + i, 0, 0))


def _ffn_call(x, mods, mrow, l, gain, wi, wo, row0):
    b, s, d = x.shape
    tm = min(FFN_TM, s)
    return pl.pallas_call(
        functools.partial(_ffn_kernel, row0=row0),
        grid=(b, s // tm),
        in_specs=[
            pl.BlockSpec((1, tm, d), lambda i, j: (i, j, 0)),
            _mod_spec(l, mrow),
            _layer((1, d), l),
            _layer((d, 2 * D_FF), l),
            _layer((D_FF, d), l),
        ],
        out_specs=pl.BlockSpec((1, tm, d), lambda i, j: (i, j, 0)),
        out_shape=jax.ShapeDtypeStruct(x.shape, f32),
        scratch_shapes=[pltpu.VMEM((tm, d), f32)],
        compiler_params=_params(("parallel", "parallel")),
        name="ffn",
    )(x, mods, gain, wi, wo)


def _group_rms(z, gmean_ref, gain):
    outs = []
    for c in range(QK_W // 256):
        zc = z[:, c * 256:(c + 1) * 256]
        hi, lo = _split_bf16(zc * zc)
        ms = _dot(hi, gmean_ref[...]) + _dot(lo, gmean_ref[...])
        outs.append(zc * lax.rsqrt(ms + EPS))
    return jnp.concatenate(outs, axis=-1) * gain


def _mixin_kernel(x_ref, mod_ref, g_ref, win_ref, gq_ref, gk_ref, gmean_ref, dc_ref, ds_ref,
                  q_ref, kt_ref, v_ref, xc_ref, xs_ref):
    x = x_ref[0]
    shift, scale = mod_ref[0, 0, 3:4, :], mod_ref[0, 0, 4:5, :]
    hb = _mod_norm(x, g_ref[...], shift, scale).astype(bf16)
    zq = _dot(hb, win_ref[:, 0:QK_W])
    q_ref[0] = (_group_rms(zq, gmean_ref, gq_ref[...]) * (ATTN_SCALE * LOG2E)).astype(bf16)
    zk = _dot(hb, win_ref[:, QK_W:2 * QK_W])
    kt_ref[0] = _group_rms(zk, gmean_ref, gk_ref[...]).T.astype(bf16)
    v = _dot(hb, win_ref[:, 2 * QK_W:2 * QK_W + V_W]).astype(bf16)
    ones = jnp.ones((v.shape[0], HEAD_W), bf16)
    for h in range(N_HEADS):
        v_ref[0, :, 2 * h * HEAD_W:(2 * h + 1) * HEAD_W] = v[:, h * HEAD_W:(h + 1) * HEAD_W]
        v_ref[0, :, (2 * h + 1) * HEAD_W:(2 * h + 2) * HEAD_W] = ones
    fb = _dot(hb, win_ref[:, 2 * QK_W + V_W:]).astype(bf16)
    for c in range(F_W // 256):
        fc = fb[:, c * 256:(c + 1) * 256]
        xc_ref[0, :, c * 256:(c + 1) * 256] = _dot(fc, dc_ref[...]).astype(bf16)
        xs_ref[0, :, c * 256:(c + 1) * 256] = _dot(fc, ds_ref[...]).astype(bf16)


def _mixin_call(x, mods, mrow, l, gain, w_in, gq, gk, gmean, dc, ds):
    b, s, d = x.shape
    tm = min(MIX_TM, s)
    row = lambda i, j: (i, j, 0)
    tok = lambda w: jax.ShapeDtypeStruct((b, s, w), bf16)
    return pl.pallas_call(
        _mixin_kernel,
        grid=(b, s // tm),
        in_specs=[
            pl.BlockSpec((1, tm, d), row),
            _mod_spec(l, mrow),
            _layer((1, d), l),
            _layer(w_in.shape[1:], l),
            _layer((1, QK_W), l),
            _layer((1, QK_W), l),
            _resident((256, 256)),
            _resident((256, 256)),
            _resident((256, 256)),
        ],
        out_specs=[
            pl.BlockSpec((1, tm, QK_W), row),
            pl.BlockSpec((1, QK_W, tm), lambda i, j: (i, 0, j)),
            pl.BlockSpec((1, tm, 2 * V_W), row),
            pl.BlockSpec((1, tm, F_W), row),
            pl.BlockSpec((1, tm, F_W), row),
        ],
        out_shape=[tok(QK_W), jax.ShapeDtypeStruct((b, QK_W, s), bf16), tok(2 * V_W), tok(F_W), tok(F_W)],
        compiler_params=_params(("parallel", "parallel")),
        name="mix_in",
    )(x, mods, gain, w_in, gq, gk, gmean, dc, ds)


def _attn_kernel(lqk_ref, q_ref, kt_ref, v_ref, bias_ref, g_ref, o_ref, s_scr, m_scr, *, n_chunks, unroll,
                 lambda_init):
    tq, tk = ATT_TQ, ATT_TK
    qt = pl.program_id(2)
    q = q_ref[0]
    lane = lax.broadcasted_iota(jnp.int32, q.shape, 1)
    zero = jnp.zeros_like(q)
    qs = jnp.concatenate([jnp.where(lane < HEAD_DIM, q, zero), jnp.where(lane >= HEAD_DIM, q, zero)], axis=0)

    m_scr[...] = jnp.full(m_scr.shape, -jnp.inf, f32)

    def score_chunk(c, carry):
        for u in range(unroll):
            kt = c * unroll + u
            k0 = pl.multiple_of(kt * tk, tk)
            s = _dot(qs, kt_ref[0, :, pl.ds(k0, tk)])
            tile = jnp.clip(kt - qt * (tq // tk) + 2, 0, N_BIAS_TILES - 1)
            for j in range(2):
                sj = s[j * tq:(j + 1) * tq] + bias_ref[j, tile]
                s_scr[j, :, pl.ds(k0, tk)] = sj
                m_scr[j] = jnp.maximum(m_scr[j], jnp.maximum(sj[:, :128], sj[:, 128:]))
        return carry

    lax.fori_loop(0, n_chunks, score_chunk, 0)

    heads = []
    for j in range(2):
        m = jnp.max(m_scr[j], axis=-1, keepdims=True)
        p = jnp.exp2(s_scr[j] - m).astype(bf16)
        ov = _dot(p, v_ref[0])
        heads.append(ov[:, :HEAD_W] / ov[:, HEAD_W:])

    lqk = lqk_ref[...]
    lam = (jnp.exp(jnp.sum(lqk[0:1] * lqk[1:2], axis=-1, keepdims=True))
           - jnp.exp(jnp.sum(lqk[2:3] * lqk[3:4], axis=-1, keepdims=True)) + lambda_init)
    o = heads[0] - lam * heads[1]
    ms = jnp.mean(o * o, axis=-1, keepdims=True)
    o_ref[0] = ((o * lax.rsqrt(ms + EPS) * g_ref[...]) * (1.0 - lambda_init)).astype(bf16)


def _attn_call(l, lqk, q, kt, vx, bias, subln, lambda_init):
    b, s, _ = q.shape
    tq, tk = ATT_TQ, ATT_TK
    unroll = min(ATT_UNROLL, max(1, s // (2 * tk)))
    assert s % (tk * unroll) == 0 and s % tq == 0
    return pl.pallas_call(
        functools.partial(_attn_kernel, n_chunks=s // (tk * unroll), unroll=unroll, lambda_init=lambda_init),
        grid=(b, N_HEADS, s // tq),
        in_specs=[
            _layer((4, HEAD_DIM), l),
            pl.BlockSpec((1, tq, HEAD_W), lambda i, h, j: (i, j, h)),
            pl.BlockSpec((1, HEAD_W, s), lambda i, h, j: (i, h, 0)),
            pl.BlockSpec((1, s, 2 * HEAD_W), lambda i, h, j: (i, 0, h)),
            pl.BlockSpec((2, N_BIAS_TILES, tq, tk), lambda i, h, j: (h, 0, 0, 0)),
            _layer((1, HEAD_W), l),
        ],
        out_specs=pl.BlockSpec((1, tq, HEAD_W), lambda i, h, j: (i, j, h)),
        out_shape=jax.ShapeDtypeStruct((b, s, V_W), bf16),
        scratch_shapes=[pltpu.VMEM((2, tq, s), f32), pltpu.VMEM((2, tq, 128), f32)],
        compiler_params=_params(("parallel", "parallel", "arbitrary")),
        name="diff_attn",
    )(lqk, q, kt, vx, bias, subln)


def _mixout_kernel(x_ref, mod_ref, o_ref, xc_ref, xs_ref, cm_ref, sm_ref, wout_ref, y_ref, rc_scr, rs_scr, *,
                   scale, half):
    @pl.when(pl.program_id(2) == 0)
    def _():
        sign = jnp.where(pl.program_id(1) == 0, 1.0, -1.0).astype(f32)
        for src, dst in ((xc_ref, rc_scr), (xs_ref, rs_scr)):
            dst[...] = (src[0, :half].astype(f32) + sign * src[0, half:].astype(f32)).astype(bf16)

    fo = (_dot(cm_ref[0], rc_scr[...]) - _dot(sm_ref[0], rs_scr[...])) * scale
    mix = _dot(o_ref[0], wout_ref[0:V_W, :]) + _dot(fo.astype(bf16), wout_ref[V_W:, :])
    y_ref[0] = x_ref[0] + mod_ref[0, 0, 5:6, :] * mix


def _mixout_call(x, mods, mrow, l, o, xc, xs, cm, sm, w_out):
    b, s, d = x.shape
    half = s // 2
    tm = min(OUT_TM, half)
    par = lambda i, p, j: (i, j, p)
    seq = lambda i, p, j: (i, 0, 0)
    y = pl.pallas_call(
        functools.partial(_mixout_kernel, scale=1.0 / math.sqrt(F_GROUP * s), half=half),
        grid=(b, 2, half // tm),
        in_specs=[
            pl.BlockSpec((1, tm, d), par),
            _mod_spec(l, mrow),
            pl.BlockSpec((1, tm, V_W), par),
            pl.BlockSpec((1, s, F_W), seq),
            pl.BlockSpec((1, s, F_W), seq),
            pl.BlockSpec((1, tm, half), lambda i, p, j: (p, j, 0)),
            pl.BlockSpec((1, tm, half), lambda i, p, j: (p, j, 0)),
            _layer(w_out.shape[1:], l),
        ],
        out_specs=pl.BlockSpec((1, tm, d), par),
        out_shape=jax.ShapeDtypeStruct((b, half, 2 * d), f32),
        scratch_shapes=[pltpu.VMEM((half, F_W), bf16), pltpu.VMEM((half, F_W), bf16)],
        compiler_params=_params(("parallel", "arbitrary", "arbitrary")),
        name="mix_out",
    )(x.reshape(b, half, 2 * d), mods, o.reshape(b, half, 2 * V_W), xc, xs, cm, sm, w_out)
    return y.reshape(b, s, d)


def _dft_kernel(ac_ref, as_ref, bc_ref, bs_ref, cm_ref, sm_ref):
    bc, bsn = bc_ref[0], bs_ref[0]
    for i in range(DFT_A_PER_STEP):
        ac, asn = ac_ref[i], as_ref[i]
        cm_ref[0, i * DFT_B:(i + 1) * DFT_B, :] = (ac * bc - asn * bsn).astype(bf16)
        sm_ref[0, i * DFT_B:(i + 1) * DFT_B, :] = (asn * bc + ac * bsn).astype(bf16)


def _dft_tables(s):
    half = s // 2
    pos = np.arange(half, dtype=np.int64)
    ang = lambda freq: 2.0 * np.pi * ((freq[..., None] * pos) % s).astype(np.float64) / s
    a = ang(np.arange(half // DFT_B, dtype=np.int64) * (2 * DFT_B))[:, None, :]
    b = ang(2 * np.arange(DFT_B, dtype=np.int64)[None, :] + np.arange(2, dtype=np.int64)[:, None])
    tab = lambda m: jnp.asarray(m.astype(np.float32))
    return tab(np.cos(a)), tab(np.sin(a)), tab(np.cos(b)), tab(np.sin(b))


def _dft_call(s):
    half = s // 2
    rows = DFT_B * DFT_A_PER_STEP
    assert half % rows == 0
    ac, asn, bc, bsn = _dft_tables(s)
    aspec = pl.BlockSpec((DFT_A_PER_STEP, 1, half), lambda p, a: (a, 0, 0))
    bspec = pl.BlockSpec((1, DFT_B, half), lambda p, a: (p, 0, 0))
    out = pl.BlockSpec((1, rows, half), lambda p, a: (p, a, 0))
    return pl.pallas_call(
        _dft_kernel,
        grid=(2, half // rows),
        in_specs=[aspec, aspec, bspec, bspec],
        out_specs=[out, out],
        out_shape=[jax.ShapeDtypeStruct((2, half, half), bf16)] * 2,
        compiler_params=_params(("parallel", "parallel")),
        name="dft_matrix",
    )(ac, asn, bc, bsn)


def _rel_bucket_np(rel):
    nb = NUM_BUCKETS // 2
    max_exact = nb // 2
    ret = (rel > 0).astype(np.int32) * nb
    n = np.abs(rel)
    nf = np.maximum(n, 1).astype(np.float32)
    ratio = np.log(nf / np.float32(max_exact)) / np.float32(math.log(REL_MAX_DISTANCE / max_exact))
    large = max_exact + (ratio * np.float32(nb - max_exact)).astype(np.int32)
    large = np.minimum(large, nb - 1)
    return (ret + np.where(n < max_exact, n, large)).astype(np.int32)


def _bucket_tiles():
    qpos = np.arange(ATT_TQ)[:, None]
    kpos = np.arange(ATT_TK)[None, :]
    return np.stack([_rel_bucket_np((d * ATT_TK + kpos) - qpos) for d in range(-2, N_BIAS_TILES - 2)])


def _bias_kernel(table_ref, bucket_ref, o_ref):
    n_maps = o_ref.shape[0]

    def rows(r, carry):
        r0 = pl.multiple_of(r * BIAS_ROWS, BIAS_ROWS)
        bucket = bucket_ref[0, pl.ds(r0, BIAS_ROWS), :]
        accs = [jnp.zeros(bucket.shape, f32) for _ in range(n_maps)]
        for bkt in range(NUM_BUCKETS):
            hit = bucket == bkt
            accs = [jnp.where(hit, table_ref[bkt, hj], acc) for hj, acc in enumerate(accs)]
        for hj, acc in enumerate(accs):
            o_ref[hj, 0, pl.ds(r0, BIAS_ROWS), :] = acc * LOG2E
        return carry

    lax.fori_loop(0, ATT_TQ // BIAS_ROWS, rows, 0)


def _bias_call(rel_bias):
    n_maps = 2 * N_HEADS
    return pl.pallas_call(
        _bias_kernel,
        grid=(N_BIAS_TILES,),
        in_specs=[
            pl.BlockSpec(memory_space=pltpu.SMEM),
            pl.BlockSpec((1, ATT_TQ, ATT_TK), lambda t: (t, 0, 0)),
        ],
        out_specs=pl.BlockSpec((n_maps, 1, ATT_TQ, ATT_TK), lambda t: (0, t, 0, 0)),
        out_shape=jax.ShapeDtypeStruct((n_maps, N_BIAS_TILES, ATT_TQ, ATT_TK), f32),
        compiler_params=_params(("arbitrary",)),
        name="rel_bias_tiles",
    )(rel_bias.reshape(NUM_BUCKETS, n_maps), jnp.asarray(_bucket_tiles()))


def _channel_dft_tables():
    idx = np.arange(F_GROUP)
    ang = 2.0 * np.pi * ((idx[:, None] * idx[None, :]) % F_GROUP) / F_GROUP
    eye = np.eye(256 // F_GROUP)
    return (jnp.asarray(np.kron(eye, np.cos(ang)), dtype=bf16), jnp.asarray(np.kron(eye, np.sin(ang)), dtype=bf16))


def _group_mean_table():
    return jnp.asarray(np.kron(np.eye(256 // HEAD_DIM), np.full((HEAD_DIM, HEAD_DIM), 1.0 / HEAD_DIM)), dtype=bf16)


def _trunk(x, mods, mrow, p, bias, cm, sm, consts):
    gmean, dc, ds = consts
    depth = p["w_in"].shape[0]
    for l in range(depth):
        lambda_init = 0.8 - 0.6 * math.exp(-0.3 * l)
        x = _ffn_call(x, mods, mrow, l, p["norm_ffn1"], p["ffn1_wi"], p["ffn1_wo"], 0)
        q, kt, vx, xc, xs = _mixin_call(x, mods, mrow, l, p["norm_mix"], p["w_in"], p["q_norm"], p["k_norm"],
                                        gmean, dc, ds)
        o = _attn_call(l, p["lambda_qk"], q, kt, vx, bias, p["subln"], lambda_init)
        x = _mixout_call(x, mods, mrow, l, o, xc, xs, cm, sm, p["w_out"])
        x = _ffn_call(x, mods, mrow, l, p["norm_ffn2"], p["ffn2_wi"], p["ffn2_wo"], 6)
    return x


def kernel(x_prompt, x_sample, c_prompt, c_sample, ada_w, ada_b, norm_ffn1, norm_mix, norm_ffn2, ffn1_wi, ffn1_wo,
           ffn2_wi, ffn2_wo, w_in, w_out, q_norm, k_norm, lambda_qk, subln, rel_bias):
    depth = ada_w.shape[0]
    nb_p, nb_s = c_prompt.shape[0], c_sample.shape[0]
    assert nb_p + nb_s <= MOD_ROWS
    c_all = jnp.concatenate([c_prompt, c_sample, jnp.zeros((MOD_ROWS - nb_p - nb_s, D_MODEL), f32)], axis=0)
    mods = _mod_call(c_all, ada_w, ada_b).reshape(depth, MOD_ROWS, N_MOD, D_MODEL)

    row = lambda a: a.reshape(depth, 1, -1)
    tile_heads = lambda a: jnp.tile(a, (1, QK_W // HEAD_DIM)).reshape(depth, 1, QK_W)
    p = dict(
        norm_ffn1=row(norm_ffn1), norm_mix=row(norm_mix), norm_ffn2=row(norm_ffn2),
        ffn1_wi=ffn1_wi.astype(bf16), ffn1_wo=ffn1_wo.astype(bf16),
        ffn2_wi=ffn2_wi.astype(bf16), ffn2_wo=ffn2_wo.astype(bf16),
        w_in=w_in.astype(bf16), w_out=w_out.astype(bf16),
        q_norm=tile_heads(q_norm), k_norm=tile_heads(k_norm),
        lambda_qk=lambda_qk, subln=row(subln),
    )
    bias = _bias_call(rel_bias)
    consts = (_group_mean_table(),) + _channel_dft_tables()

    outs = []
    for x, mrow in ((x_prompt, 0), (x_sample, nb_p)):
        cm, sm = _dft_call(x.shape[1])
        outs.append(_trunk(x, mods, mrow, p, bias, cm, sm, consts))
    return tuple(outs)
```

```python
import functools
import math

import numpy as np
import jax
import jax.numpy as jnp
from jax import lax
from jax.experimental import pallas as pl
from jax.experimental.pallas import tpu as pltpu

D_MODEL = 1024
HEAD_DIM = 64
N_HEADS = 4
HEAD_W = 2 * HEAD_DIM
QK_W = N_HEADS * HEAD_W
V_W = QK_W
F_W = 512
F_GROUP = 64
D_FF = 2816
N_MOD = 9
NUM_BUCKETS = 32
REL_MAX_DISTANCE = 128
EPS = 1e-6
ATTN_SCALE = HEAD_DIM ** -0.5

V7X_VMEM_LIMIT_BYTES = 56 * 1024 * 1024
MOD_ROWS = 16
MOD_TN = 1024
FFN_TM = 512
FFN_CK = 256
MIX_TM = 512
ATT_TQ = 512
ATT_TK = 256
ATT_UNROLL = 8
N_BIAS_TILES = ATT_TQ // ATT_TK + 4
BIAS_ROWS = 16
LOG2E = math.log2(math.e)
OUT_TM = 256
DFT_B = 64
DFT_A_PER_STEP = 4

f32 = jnp.float32
bf16 = jnp.bfloat16


def _dot(a, b):
    return jnp.dot(a, b, preferred_element_type=f32)


def _params(sem, vmem=V7X_VMEM_LIMIT_BYTES):
    return pltpu.CompilerParams(dimension_semantics=sem, vmem_limit_bytes=vmem)


def _resident(shape):
    return pl.BlockSpec(shape, lambda *_: (0,) * len(shape), pipeline_mode=pl.Buffered(1))


def _layer(shape, l):
    return pl.BlockSpec((None,) + tuple(shape), lambda *_: (l,) + (0,) * len(shape), pipeline_mode=pl.Buffered(1))


def _split_bf16(a):
    hi = a.astype(bf16)
    lo = (a - hi.astype(f32)).astype(bf16)
    return hi, lo


def _mod_norm(x, gain, shift, scale):
    ms = jnp.mean(x * x, axis=-1, keepdims=True)
    return (x * lax.rsqrt(ms + EPS) * gain) * (1.0 + scale) + shift


def _mod_kernel(c_ref, w_ref, b_ref, o_ref):
    c = c_ref[...]
    sc = c * (1.0 / (1.0 + jnp.exp(-c)))
    s_hi, s_lo = _split_bf16(sc)
    w_hi, w_lo = _split_bf16(w_ref[0])
    o_ref[0] = _dot(s_hi, w_hi) + _dot(s_hi, w_lo) + _dot(s_lo, w_hi) + b_ref[0]


def _mod_call(c_all, ada_w, ada_b):
    depth, _, n = ada_w.shape
    return pl.pallas_call(
        _mod_kernel,
        grid=(depth, n // MOD_TN),
        in_specs=[
            pl.BlockSpec((MOD_ROWS, D_MODEL), lambda l, j: (0, 0)),
            pl.BlockSpec((1, D_MODEL, MOD_TN), lambda l, j: (l, 0, j)),
            pl.BlockSpec((1, 1, MOD_TN), lambda l, j: (l, 0, j)),
        ],
        out_specs=pl.BlockSpec((1, MOD_ROWS, MOD_TN), lambda l, j: (l, 0, j)),
        out_shape=jax.ShapeDtypeStruct((depth, MOD_ROWS, n), f32),
        compiler_params=_params(("arbitrary", "arbitrary")),
        name="adaln_mod",
    )(c_all, ada_w, ada_b.reshape(depth, 1, n))


def _ffn_kernel(x_ref, mod_ref, g_ref, wi_ref, wo_ref, o_ref, acc_ref, *, row0):
    x = x_ref[0]
    shift, scale, gate = (mod_ref[0, 0, row0 + i:row0 + i + 1, :] for i in range(3))
    hb = _mod_norm(x, g_ref[...], shift, scale).astype(bf16)
    for c in range(D_FF // FFN_CK):
        g = _dot(hb, wi_ref[:, c * FFN_CK:(c + 1) * FFN_CK])
        u = _dot(hb, wi_ref[:, D_FF + c * FFN_CK:D_FF + (c + 1) * FFN_CK])
        a = (g * (1.0 / (1.0 + jnp.exp(-g))) * u).astype(bf16)
        part = _dot(a, wo_ref[c * FFN_CK:(c + 1) * FFN_CK, :])
        if c == 0:
            acc_ref[...] = part
        else:
            acc_ref[...] += part
    o_ref[0] = x + (0.5 * gate) * acc_ref[...]


def _mod_spec(l, row0):
    return pl.BlockSpec((1, 1, N_MOD, D_MODEL), lambda i, *_: (l, row0 + i, 0, 0))


def _ffn_call(x, mods, mrow, l, gain, wi, wo, row0):
    b, s, d = x.shape
    tm = min(FFN_TM, s)
    return pl.pallas_call(
        functools.partial(_ffn_kernel, row0=row0),
        grid=(b, s // tm),
        in_specs=[
            pl.BlockSpec((1, tm, d), lambda i, j: (i, j, 0)),
            _mod_spec(l, mrow),
            _layer((1, d), l),
            _layer((d, 2 * D_FF), l),
            _layer((D_FF, d), l),
        ],
        out_specs=pl.BlockSpec((1, tm, d), lambda i, j: (i, j, 0)),
        out_shape=jax.ShapeDtypeStruct(x.shape, f32),
        scratch_shapes=[pltpu.VMEM((tm, d), f32)],
        compiler_params=_params(("parallel", "parallel")),
        name="ffn",
    )(x, mods, gain, wi, wo)


def _group_rms(z, gmean_ref, gain):
    outs = []
    for c in range(QK_W // 256):
        zc = z[:, c * 256:(c + 1) * 256]
        hi, lo = _split_bf16(zc * zc)
        ms = _dot(hi, gmean_ref[...]) + _dot(lo, gmean_ref[...])
        outs.append(zc * lax.rsqrt(ms + EPS))
    return jnp.concatenate(outs, axis=-1) * gain


def _mixin_kernel(x_ref, mod_ref, g_ref, win_ref, gq_ref, gk_ref, gmean_ref, dc_ref, ds_ref,
                  q_ref, kt_ref, v_ref, xc_ref, xs_ref):
    x = x_ref[0]
    shift, scale = mod_ref[0, 0, 3:4, :], mod_ref[0, 0, 4:5, :]
    hb = _mod_norm(x, g_ref[...], shift, scale).astype(bf16)
    zq = _dot(hb, win_ref[:, 0:QK_W])
    q_ref[0] = (_group_rms(zq, gmean_ref, gq_ref[...]) * (ATTN_SCALE * LOG2E)).astype(bf16)
    zk = _dot(hb, win_ref[:, QK_W:2 * QK_W])
    kt_ref[0] = _group_rms(zk, gmean_ref, gk_ref[...]).T.astype(bf16)
    v = _dot(hb, win_ref[:, 2 * QK_W:2 * QK_W + V_W]).astype(bf16)
    ones = jnp.ones((v.shape[0], HEAD_W), bf16)
    for h in range(N_HEADS):
        v_ref[0, :, 2 * h * HEAD_W:(2 * h + 1) * HEAD_W] = v[:, h * HEAD_W:(h + 1) * HEAD_W]
        v_ref[0, :, (2 * h + 1) * HEAD_W:(2 * h + 2) * HEAD_W] = ones
    fb = _dot(hb, win_ref[:, 2 * QK_W + V_W:]).astype(bf16)
    for c in range(F_W // 256):
        fc = fb[:, c * 256:(c + 1) * 256]
        xc_ref[0, :, c * 256:(c + 1) * 256] = _dot(fc, dc_ref[...]).astype(bf16)
        xs_ref[0, :, c * 256:(c + 1) * 256] = _dot(fc, ds_ref[...]).astype(bf16)


def _mixin_call(x, mods, mrow, l, gain, w_in, gq, gk, gmean, dc, ds):
    b, s, d = x.shape
    tm = min(MIX_TM, s)
    row = lambda i, j: (i, j, 0)
    tok = lambda w: jax.ShapeDtypeStruct((b, s, w), bf16)
    return pl.pallas_call(
        _mixin_kernel,
        grid=(b, s // tm),
        in_specs=[
            pl.BlockSpec((1, tm, d), row),
            _mod_spec(l, mrow),
            _layer((1, d), l),
            _layer(w_in.shape[1:], l),
            _layer((1, QK_W), l),
            _layer((1, QK_W), l),
            _resident((256, 256)),
            _resident((256, 256)),
            _resident((256, 256)),
        ],
        out_specs=[
            pl.BlockSpec((1, tm, QK_W), row),
            pl.BlockSpec((1, QK_W, tm), lambda i, j: (i, 0, j)),
            pl.BlockSpec((1, tm, 2 * V_W), row),
            pl.BlockSpec((1, tm, F_W), row),
            pl.BlockSpec((1, tm, F_W), row),
        ],
        out_shape=[tok(QK_W), jax.ShapeDtypeStruct((b, QK_W, s), bf16), tok(2 * V_W), tok(F_W), tok(F_W)],
        compiler_params=_params(("parallel", "parallel")),
        name="mix_in",
    )(x, mods, gain, w_in, gq, gk, gmean, dc, ds)


def _attn_kernel(lqk_ref, q_ref, kt_ref, v_ref, bias_ref, g_ref, o_ref, s_scr, m_scr, *, n_chunks, unroll,
                 lambda_init):
    tq, tk = ATT_TQ, ATT_TK
    qt = pl.program_id(2)
    q = q_ref[0]
    lane = lax.broadcasted_iota(jnp.int32, q.shape, 1)
    zero = jnp.zeros_like(q)
    qs = jnp.concatenate([jnp.where(lane < HEAD_DIM, q, zero), jnp.where(lane >= HEAD_DIM, q, zero)], axis=0)

    m_scr[...] = jnp.full(m_scr.shape, -jnp.inf, f32)

    def score_chunk(c, carry):
        for u in range(unroll):
            kt = c * unroll + u
            k0 = pl.multiple_of(kt * tk, tk)
            s = _dot(qs, kt_ref[0, :, pl.ds(k0, tk)])
            tile = jnp.clip(kt - qt * (tq // tk) + 2, 0, N_BIAS_TILES - 1)
            for j in range(2):
                sj = s[j * tq:(j + 1) * tq] + bias_ref[j, tile]
                s_scr[j, :, pl.ds(k0, tk)] = sj
                m_scr[j] = jnp.maximum(m_scr[j], jnp.maximum(sj[:, :128], sj[:, 128:]))
        return carry

    lax.fori_loop(0, n_chunks, score_chunk, 0)

    heads = []
    for j in range(2):
        m = jnp.max(m_scr[j], axis=-1, keepdims=True)
        p = jnp.exp2(s_scr[j] - m).astype(bf16)
        ov = _dot(p, v_ref[0])
        heads.append(ov[:, :HEAD_W] / ov[:, HEAD_W:])

    lqk = lqk_ref[...]
    lam = (jnp.exp(jnp.sum(lqk[0:1] * lqk[1:2], axis=-1, keepdims=True))
           - jnp.exp(jnp.sum(lqk[2:3] * lqk[3:4], axis=-1, keepdims=True)) + lambda_init)
    o = heads[0] - lam * heads[1]
    ms = jnp.mean(o * o, axis=-1, keepdims=True)
    o_ref[0] = ((o * lax.rsqrt(ms + EPS) * g_ref[...]) * (1.0 - lambda_init)).astype(bf16)


def _attn_call(l, lqk, q, kt, vx, bias, subln, lambda_init):
    b, s, _ = q.shape
    tq, tk = ATT_TQ, ATT_TK
    unroll = min(ATT_UNROLL, max(1, s // (2 * tk)))
    assert s % (tk * unroll) == 0 and s % tq == 0
    return pl.pallas_call(
        functools.partial(_attn_kernel, n_chunks=s // (tk * unroll), unroll=unroll, lambda_init=lambda_init),
        grid=(b, N_HEADS, s // tq),
        in_specs=[
            _layer((4, HEAD_DIM), l),
            pl.BlockSpec((1, tq, HEAD_W), lambda i, h, j: (i, j, h)),
            pl.BlockSpec((1, HEAD_W, s), lambda i, h, j: (i, h, 0)),
            pl.BlockSpec((1, s, 2 * HEAD_W), lambda i, h, j: (i, 0, h)),
            pl.BlockSpec((2, N_BIAS_TILES, tq, tk), lambda i, h, j: (h, 0, 0, 0)),
            _layer((1, HEAD_W), l),
        ],
        out_specs=pl.BlockSpec((1, tq, HEAD_W), lambda i, h, j: (i, j, h)),
        out_shape=jax.ShapeDtypeStruct((b, s, V_W), bf16),
        scratch_shapes=[pltpu.VMEM((2, tq, s), f32), pltpu.VMEM((2, tq, 128), f32)],
        compiler_params=_params(("parallel", "parallel", "arbitrary")),
        name="diff_attn",
    )(lqk, q, kt, vx, bias, subln)


def _mixout_kernel(x_ref, mod_ref, o_ref, xc_ref, xs_ref, cm_ref, sm_ref, wout_ref, y_ref, rc_scr, rs_scr, fo_scr, *,
                   scale, half):
    @pl.when(pl.program_id(1) == 0)
    def _():
        for src, dst in ((xc_ref, rc_scr), (xs_ref, rs_scr)):
            lo, hi = src[0, :half].astype(f32), src[0, half:].astype(f32)
            dst[0] = (lo + hi).astype(bf16)
            dst[1] = (lo - hi).astype(bf16)

    tm = cm_ref.shape[1]
    for p in range(2):
        fo = (_dot(cm_ref[p], rc_scr[p]) - _dot(sm_ref[p], rs_scr[p])) * scale
        for c in range(F_W // 128):
            fo_scr[c, pl.ds(p, tm, stride=2), :] = fo[:, c * 128:(c + 1) * 128]
    fo = jnp.concatenate([fo_scr[c] for c in range(F_W // 128)], axis=-1).astype(bf16)
    mix = _dot(o_ref[0], wout_ref[0:V_W, :]) + _dot(fo, wout_ref[V_W:, :])
    y_ref[0] = x_ref[0] + mod_ref[0, 0, 5:6, :] * mix


def _mixout_call(x, mods, mrow, l, o, xc, xs, cm, sm, w_out):
    b, s, d = x.shape
    half = s // 2
    tm = min(OUT_TM, half)
    row = lambda i, j: (i, j, 0)
    seq = lambda i, j: (i, 0, 0)
    return pl.pallas_call(
        functools.partial(_mixout_kernel, scale=1.0 / math.sqrt(F_GROUP * s), half=half),
        grid=(b, half // tm),
        in_specs=[
            pl.BlockSpec((1, 2 * tm, d), row),
            _mod_spec(l, mrow),
            pl.BlockSpec((1, 2 * tm, V_W), row),
            pl.BlockSpec((1, s, F_W), seq),
            pl.BlockSpec((1, s, F_W), seq),
            pl.BlockSpec((2, tm, half), lambda i, j: (0, j, 0)),
            pl.BlockSpec((2, tm, half), lambda i, j: (0, j, 0)),
            _layer(w_out.shape[1:], l),
        ],
        out_specs=pl.BlockSpec((1, 2 * tm, d), row),
        out_shape=jax.ShapeDtypeStruct(x.shape, f32),
        scratch_shapes=[pltpu.VMEM((2, half, F_W), bf16), pltpu.VMEM((2, half, F_W), bf16),
                        pltpu.VMEM((F_W // 128, 2 * tm, 128), f32)],
        compiler_params=_params(("parallel", "arbitrary")),
        name="mix_out",
    )(x, mods, o, xc, xs, cm, sm, w_out)


def _dft_kernel(ac_ref, as_ref, bc_ref, bs_ref, cm_ref, sm_ref):
    bc, bsn = bc_ref[0], bs_ref[0]
    for i in range(DFT_A_PER_STEP):
        ac, asn = ac_ref[i], as_ref[i]
        cm_ref[0, i * DFT_B:(i + 1) * DFT_B, :] = (ac * bc - asn * bsn).astype(bf16)
        sm_ref[0, i * DFT_B:(i + 1) * DFT_B, :] = (asn * bc + ac * bsn).astype(bf16)


def _dft_tables(s):
    half = s // 2
    pos = np.arange(half, dtype=np.int64)
    ang = lambda freq: 2.0 * np.pi * ((freq[..., None] * pos) % s).astype(np.float64) / s
    a = ang(np.arange(half // DFT_B, dtype=np.int64) * (2 * DFT_B))[:, None, :]
    b = ang(2 * np.arange(DFT_B, dtype=np.int64)[None, :] + np.arange(2, dtype=np.int64)[:, None])
    tab = lambda m: jnp.asarray(m.astype(np.float32))
    return tab(np.cos(a)), tab(np.sin(a)), tab(np.cos(b)), tab(np.sin(b))


def _dft_call(s):
    half = s // 2
    rows = DFT_B * DFT_A_PER_STEP
    assert half % rows == 0
    ac, asn, bc, bsn = _dft_tables(s)
    aspec = pl.BlockSpec((DFT_A_PER_STEP, 1, half), lambda p, a: (a, 0, 0))
    bspec = pl.BlockSpec((1, DFT_B, half), lambda p, a: (p, 0, 0))
    out = pl.BlockSpec((1, rows, half), lambda p, a: (p, a, 0))
    return pl.pallas_call(
        _dft_kernel,
        grid=(2, half // rows),
        in_specs=[aspec, aspec, bspec, bspec],
        out_specs=[out, out],
        out_shape=[jax.ShapeDtypeStruct((2, half, half), bf16)] * 2,
        compiler_params=_params(("parallel", "parallel")),
        name="dft_matrix",
    )(ac, asn, bc, bsn)


def _rel_bucket_np(rel):
    nb = NUM_BUCKETS // 2
    max_exact = nb // 2
    ret = (rel > 0).astype(np.int32) * nb
    n = np.abs(rel)
    nf = np.maximum(n, 1).astype(np.float32)
    ratio = np.log(nf / np.float32(max_exact)) / np.float32(math.log(REL_MAX_DISTANCE / max_exact))
    large = max_exact + (ratio * np.float32(nb - max_exact)).astype(np.int32)
    large = np.minimum(large, nb - 1)
    return (ret + np.where(n < max_exact, n, large)).astype(np.int32)


def _bucket_tiles():
    qpos = np.arange(ATT_TQ)[:, None]
    kpos = np.arange(ATT_TK)[None, :]
    return np.stack([_rel_bucket_np((d * ATT_TK + kpos) - qpos) for d in range(-2, N_BIAS_TILES - 2)])


def _bias_kernel(table_ref, bucket_ref, o_ref):
    n_maps = o_ref.shape[0]

    def rows(r, carry):
        r0 = pl.multiple_of(r * BIAS_ROWS, BIAS_ROWS)
        bucket = bucket_ref[0, pl.ds(r0, BIAS_ROWS), :]
        accs = [jnp.zeros(bucket.shape, f32) for _ in range(n_maps)]
        for bkt in range(NUM_BUCKETS):
            hit = bucket == bkt
            accs = [jnp.where(hit, table_ref[bkt, hj], acc) for hj, acc in enumerate(accs)]
        for hj, acc in enumerate(accs):
            o_ref[hj, 0, pl.ds(r0, BIAS_ROWS), :] = acc * LOG2E
        return carry

    lax.fori_loop(0, ATT_TQ // BIAS_ROWS, rows, 0)


def _bias_call(rel_bias):
    n_maps = 2 * N_HEADS
    return pl.pallas_call(
        _bias_kernel,
        grid=(N_BIAS_TILES,),
        in_specs=[
            pl.BlockSpec(memory_space=pltpu.SMEM),
            pl.BlockSpec((1, ATT_TQ, ATT_TK), lambda t: (t, 0, 0)),
        ],
        out_specs=pl.BlockSpec((n_maps, 1, ATT_TQ, ATT_TK), lambda t: (0, t, 0, 0)),
        out_shape=jax.ShapeDtypeStruct((n_maps, N_BIAS_TILES, ATT_TQ, ATT_TK), f32),
        compiler_params=_params(("arbitrary",)),
        name="rel_bias_tiles",
    )(rel_bias.reshape(NUM_BUCKETS, n_maps), jnp.asarray(_bucket_tiles()))


def _channel_dft_tables():
    idx = np.arange(F_GROUP)
    ang = 2.0 * np.pi * ((idx[:, None] * idx[None, :]) % F_GROUP) / F_GROUP
    eye = np.eye(256 // F_GROUP)
    return (jnp.asarray(np.kron(eye, np.cos(ang)), dtype=bf16), jnp.asarray(np.kron(eye, np.sin(ang)), dtype=bf16))


def _group_mean_table():
    return jnp.asarray(np.kron(np.eye(256 // HEAD_DIM), np.full((HEAD_DIM, HEAD_DIM), 1.0 / HEAD_DIM)), dtype=bf16)


def _trunk(x, mods, mrow, p, bias, cm, sm, consts):
    gmean, dc, ds = consts
    depth = p["w_in"].shape[0]
    for l in range(depth):
        lambda_init = 0.8 - 0.6 * math.exp(-0.3 * l)
        x = _ffn_call(x, mods, mrow, l, p["norm_ffn1"], p["ffn1_wi"], p["ffn1_wo"], 0)
        q, kt, vx, xc, xs = _mixin_call(x, mods, mrow, l, p["norm_mix"], p["w_in"], p["q_norm"], p["k_norm"],
                                        gmean, dc, ds)
        o = _attn_call(l, p["lambda_qk"], q, kt, vx, bias, p["subln"], lambda_init)
        x = _mixout_call(x, mods, mrow, l, o, xc, xs, cm, sm, p["w_out"])
        x = _ffn_call(x, mods, mrow, l, p["norm_ffn2"], p["ffn2_wi"], p["ffn2_wo"], 6)
    return x


def kernel(x_prompt, x_sample, c_prompt, c_sample, ada_w, ada_b, norm_ffn1, norm_mix, norm_ffn2, ffn1_wi, ffn1_wo,
           ffn2_wi, ffn2_wo, w_in, w_out, q_norm, k_norm, lambda_qk, subln, rel_bias):
    depth = ada_w.shape[0]
    nb_p, nb_s = c_prompt.shape[0], c_sample.shape[0]
    assert nb_p + nb_s <= MOD_ROWS
    c_all = jnp.concatenate([c_prompt, c_sample, jnp.zeros((MOD_ROWS - nb_p - nb_s, D_MODEL), f32)], axis=0)
    mods = _mod_call(c_all, ada_w, ada_b).reshape(depth, MOD_ROWS, N_MOD, D_MODEL)

    row = lambda a: a.reshape(depth, 1, -1)
    tile_heads = lambda a: jnp.tile(a, (1, QK_W // HEAD_DIM)).reshape(depth, 1, QK_W)
    p = dict(
        norm_ffn1=row(norm_ffn1), norm_mix=row(norm_mix), norm_ffn2=row(norm_ffn2),
        ffn1_wi=ffn1_wi.astype(bf16), ffn1_wo=ffn1_wo.astype(bf16),
        ffn2_wi=ffn2_wi.astype(bf16), ffn2_wo=ffn2_wo.astype(bf16),
        w_in=w_in.astype(bf16), w_out=w_out.astype(bf16),
        q_norm=tile_heads(q_norm), k_norm=tile_heads(k_norm),
        lambda_qk=lambda_qk, subln=row(subln),
    )
    bias = _bias_call(rel_bias)
    consts = (_group_mean_table(),) + _channel_dft_tables()

    outs = []
    for x, mrow in ((x_prompt, 0), (x_sample, nb_p)):
        cm, sm = _dft_call(x.shape[1])
        outs.append(_trunk(x, mods, mrow, p, bias, cm, sm, consts))
    return tuple(outs)
```

```python
import functools
import math

import numpy as np
import jax
import jax.numpy as jnp
from jax import lax
from jax.experimental import pallas as pl
from jax.experimental.pallas import tpu as pltpu

D_MODEL = 1024
HEAD_DIM = 64
N_HEADS = 4
HEAD_W = 2 * HEAD_DIM
QK_W = N_HEADS * HEAD_W
V_W = QK_W
F_W = 512
F_GROUP = 64
D_FF = 2816
N_MOD = 9
NUM_BUCKETS = 32
REL_MAX_DISTANCE = 128
EPS = 1e-6
ATTN_SCALE = HEAD_DIM ** -0.5

V7X_VMEM_LIMIT_BYTES = 56 * 1024 * 1024
MOD_ROWS = 16
MOD_TN = 1024
FFN_TM = 512
FFN_CK = 256
MIX_TM = 512
ATT_TQ = 512
ATT_TK = 256
ATT_UNROLL = 8
N_BIAS_TILES = ATT_TQ // ATT_TK + 4
BIAS_ROWS = 16
LOG2E = math.log2(math.e)
OUT_TM = 256
DFT_B = 64
DFT_A_PER_STEP = 4

f32 = jnp.float32
bf16 = jnp.bfloat16


def _dot(a, b):
    return jnp.dot(a, b, preferred_element_type=f32)


def _params(sem, vmem=V7X_VMEM_LIMIT_BYTES):
    return pltpu.CompilerParams(dimension_semantics=sem, vmem_limit_bytes=vmem)


def _resident(shape):
    return pl.BlockSpec(shape, lambda *_: (0,) * len(shape), pipeline_mode=pl.Buffered(1))


def _layer(shape, l):
    return pl.BlockSpec((None,) + tuple(shape), lambda *_: (l,) + (0,) * len(shape), pipeline_mode=pl.Buffered(1))


def _split_bf16(a):
    hi = a.astype(bf16)
    lo = (a - hi.astype(f32)).astype(bf16)
    return hi, lo


def _mod_norm(x, gain, shift, scale):
    ms = jnp.mean(x * x, axis=-1, keepdims=True)
    return (x * lax.rsqrt(ms + EPS) * gain) * (1.0 + scale) + shift


def _mod_kernel(c_ref, w_ref, b_ref, o_ref):
    c = c_ref[...]
    sc = c * (1.0 / (1.0 + jnp.exp(-c)))
    s_hi, s_lo = _split_bf16(sc)
    w_hi, w_lo = _split_bf16(w_ref[0])
    o_ref[0] = _dot(s_hi, w_hi) + _dot(s_hi, w_lo) + _dot(s_lo, w_hi) + b_ref[0]


def _mod_call(c_all, ada_w, ada_b):
    depth, _, n = ada_w.shape
    return pl.pallas_call(
        _mod_kernel,
        grid=(depth, n // MOD_TN),
        in_specs=[
            pl.BlockSpec((MOD_ROWS, D_MODEL), lambda l, j: (0, 0)),
            pl.BlockSpec((1, D_MODEL, MOD_TN), lambda l, j: (l, 0, j)),
            pl.BlockSpec((1, 1, MOD_TN), lambda l, j: (l, 0, j)),
        ],
        out_specs=pl.BlockSpec((1, MOD_ROWS, MOD_TN), lambda l, j: (l, 0, j)),
        out_shape=jax.ShapeDtypeStruct((depth, MOD_ROWS, n), f32),
        compiler_params=_params(("arbitrary", "arbitrary")),
        name="adaln_mod",
    )(c_all, ada_w, ada_b.reshape(depth, 1, n))


def _ffn_kernel(x_ref, mod_ref, g_ref, wi_ref, wo_ref, o_ref, acc_ref, *, row0):
    x = x_ref[0]
    shift, scale, gate = (mod_ref[0, 0, row0 + i:row0 + i + 1, :] for i in range(3))
    hb = _mod_norm(x, g_ref[...], shift, scale).astype(bf16)
    for c in range(D_FF // FFN_CK):
        g = _dot(hb, wi_ref[:, c * FFN_CK:(c + 1) * FFN_CK])
        u = _dot(hb, wi_ref[:, D_FF + c * FFN_CK:D_FF + (c + 1) * FFN_CK])
        a = (g * (1.0 / (1.0 + jnp.exp(-g))) * u).astype(bf16)
        part = _dot(a, wo_ref[c * FFN_CK:(c + 1) * FFN_CK, :])
        if c == 0:
            acc_ref[...] = part
        else:
            acc_ref[...] += part
    o_ref[0] = x + (0.5 * gate) * acc_ref[...]


def _mod_spec(l, row0):
    return pl.BlockSpec((1, 1, N_MOD, D_MODEL), lambda i, *_: (l, row0 + i, 0, 0))


def _ffn_call(x, mods, mrow, l, gain, wi, wo, row0):
    b, s, d = x.shape
    tm = min(FFN_TM, s)
    return pl.pallas_call(
        functools.partial(_ffn_kernel, row0=row0),
        grid=(b, s // tm),
        in_specs=[
            pl.BlockSpec((1, tm, d), lambda i, j: (i, j, 0)),
            _mod_spec(l, mrow),
            _layer((1, d), l),
            _layer((d, 2 * D_FF), l),
            _layer((D_FF, d), l),
        ],
        out_specs=pl.BlockSpec((1, tm, d), lambda i, j: (i, j, 0)),
        out_shape=jax.ShapeDtypeStruct(x.shape, f32),
        scratch_shapes=[pltpu.VMEM((tm, d), f32)],
        compiler_params=_params(("parallel", "parallel")),
        name="ffn",
    )(x, mods, gain, wi, wo)


def _group_rms(z, gmean_ref, gain):
    outs = []
    for c in range(QK_W // 256):
        zc = z[:, c * 256:(c + 1) * 256]
        hi, lo = _split_bf16(zc * zc)
        ms = _dot(hi, gmean_ref[...]) + _dot(lo, gmean_ref[...])
        outs.append(zc * lax.rsqrt(ms + EPS))
    return jnp.concatenate(outs, axis=-1) * gain


def _mixin_kernel(x_ref, mod_ref, g_ref, win_ref, gq_ref, gk_ref, gmean_ref, dc_ref, ds_ref,
                  q_ref, kt_ref, v_ref, xc_ref, xs_ref):
    x = x_ref[0]
    shift, scale = mod_ref[0, 0, 3:4, :], mod_ref[0, 0, 4:5, :]
    hb = _mod_norm(x, g_ref[...], shift, scale).astype(bf16)
    zq = _dot(hb, win_ref[:, 0:QK_W])
    q_ref[0] = (_group_rms(zq, gmean_ref, gq_ref[...]) * (ATTN_SCALE * LOG2E)).astype(bf16)
    zk = _dot(hb, win_ref[:, QK_W:2 * QK_W])
    kt_ref[0] = _group_rms(zk, gmean_ref, gk_ref[...]).T.astype(bf16)
    v = _dot(hb, win_ref[:, 2 * QK_W:2 * QK_W + V_W]).astype(bf16)
    ones = jnp.ones((v.shape[0], HEAD_W), bf16)
    for h in range(N_HEADS):
        v_ref[0, :, 2 * h * HEAD_W:(2 * h + 1) * HEAD_W] = v[:, h * HEAD_W:(h + 1) * HEAD_W]
        v_ref[0, :, (2 * h + 1) * HEAD_W:(2 * h + 2) * HEAD_W] = ones
    fb = _dot(hb, win_ref[:, 2 * QK_W + V_W:]).astype(bf16)
    for c in range(F_W // 256):
        fc = fb[:, c * 256:(c + 1) * 256]
        xc_ref[0, :, c * 256:(c + 1) * 256] = _dot(fc, dc_ref[...]).astype(bf16)
        xs_ref[0, :, c * 256:(c + 1) * 256] = _dot(fc, ds_ref[...]).astype(bf16)


def _mixin_call(x, mods, mrow, l, gain, w_in, gq, gk, gmean, dc, ds):
    b, s, d = x.shape
    tm = min(MIX_TM, s)
    row = lambda i, j: (i, j, 0)
    tok = lambda w: jax.ShapeDtypeStruct((b, s, w), bf16)
    return pl.pallas_call(
        _mixin_kernel,
        grid=(b, s // tm),
        in_specs=[
            pl.BlockSpec((1, tm, d), row),
            _mod_spec(l, mrow),
            _layer((1, d), l),
            _layer(w_in.shape[1:], l),
            _layer((1, QK_W), l),
            _layer((1, QK_W), l),
            _resident((256, 256)),
            _resident((256, 256)),
            _resident((256, 256)),
        ],
        out_specs=[
            pl.BlockSpec((1, tm, QK_W), row),
            pl.BlockSpec((1, QK_W, tm), lambda i, j: (i, 0, j)),
            pl.BlockSpec((1, tm, 2 * V_W), row),
            pl.BlockSpec((1, tm, F_W), row),
            pl.BlockSpec((1, tm, F_W), row),
        ],
        out_shape=[tok(QK_W), jax.ShapeDtypeStruct((b, QK_W, s), bf16), tok(2 * V_W), tok(F_W), tok(F_W)],
        compiler_params=_params(("parallel", "parallel")),
        name="mix_in",
    )(x, mods, gain, w_in, gq, gk, gmean, dc, ds)


def _attn_kernel(lqk_ref, q_ref, kt_ref, v_ref, bias_ref, g_ref, o_ref, s_scr, m_scr, qs_scr, *, n_tiles,
                 lambda_init):
    tq, tk = ATT_TQ, ATT_TK
    qt = pl.program_id(2)
    q = q_ref[0]
    lane = lax.broadcasted_iota(jnp.int32, q.shape, 1)
    zero = jnp.zeros_like(q)
    qs_scr[0] = jnp.where(lane < HEAD_DIM, q, zero)
    qs_scr[1] = jnp.where(lane >= HEAD_DIM, q, zero)

    def score_map(j, carry):
        qj = qs_scr[j]
        m_run = jnp.full((tq, 128), -jnp.inf, bf16)
        for kt in range(n_tiles):
            tile = jnp.clip(kt - qt * (tq // tk) + 2, 0, N_BIAS_TILES - 1)
            sj = _dot(qj, kt_ref[0, :, kt * tk:(kt + 1) * tk]) + bias_ref[j, tile]
            s_scr[j, :, kt * tk:(kt + 1) * tk] = sj
            m_run = jnp.maximum(m_run, jnp.maximum(sj[:, :128], sj[:, 128:]).astype(bf16))
        m_scr[j] = m_run.astype(f32)
        return carry

    lax.fori_loop(0, 2, score_map, 0)

    heads = []
    for j in range(2):
        m = jnp.max(m_scr[j], axis=-1, keepdims=True)
        p = jnp.exp2((s_scr[j] - m).astype(bf16))
        ov = _dot(p, v_ref[0])
        heads.append(ov[:, :HEAD_W] / ov[:, HEAD_W:])

    lqk = lqk_ref[...]
    lam = (jnp.exp(jnp.sum(lqk[0:1] * lqk[1:2], axis=-1, keepdims=True))
           - jnp.exp(jnp.sum(lqk[2:3] * lqk[3:4], axis=-1, keepdims=True)) + lambda_init)
    o = heads[0] - lam * heads[1]
    ms = jnp.mean(o * o, axis=-1, keepdims=True)
    o_ref[0] = ((o * lax.rsqrt(ms + EPS) * g_ref[...]) * (1.0 - lambda_init)).astype(bf16)


def _attn_call(l, lqk, q, kt, vx, bias, subln, lambda_init):
    b, s, _ = q.shape
    tq, tk = ATT_TQ, ATT_TK
    assert s % tk == 0 and s % tq == 0
    return pl.pallas_call(
        functools.partial(_attn_kernel, n_tiles=s // tk, lambda_init=lambda_init),
        grid=(b, N_HEADS, s // tq),
        in_specs=[
            _layer((4, HEAD_DIM), l),
            pl.BlockSpec((1, tq, HEAD_W), lambda i, h, j: (i, j, h)),
            pl.BlockSpec((1, HEAD_W, s), lambda i, h, j: (i, h, 0)),
            pl.BlockSpec((1, s, 2 * HEAD_W), lambda i, h, j: (i, 0, h)),
            pl.BlockSpec((2, N_BIAS_TILES, tq, tk), lambda i, h, j: (h, 0, 0, 0)),
            _layer((1, HEAD_W), l),
        ],
        out_specs=pl.BlockSpec((1, tq, HEAD_W), lambda i, h, j: (i, j, h)),
        out_shape=jax.ShapeDtypeStruct((b, s, V_W), bf16),
        scratch_shapes=[pltpu.VMEM((2, tq, s), f32), pltpu.VMEM((2, tq, 128), f32),
                        pltpu.VMEM((2, tq, HEAD_W), bf16)],
        compiler_params=_params(("parallel", "parallel", "arbitrary")),
        name="diff_attn",
    )(lqk, q, kt, vx, bias, subln)


def _mixout_kernel(x_ref, mod_ref, o_ref, xc_ref, xs_ref, cm_ref, sm_ref, wout_ref, y_ref, rc_scr, rs_scr, fo_scr, *,
                   scale, half):
    @pl.when(pl.program_id(1) == 0)
    def _():
        for src, dst in ((xc_ref, rc_scr), (xs_ref, rs_scr)):
            lo, hi = src[0, :half].astype(f32), src[0, half:].astype(f32)
            dst[0] = (lo + hi).astype(bf16)
            dst[1] = (lo - hi).astype(bf16)

    tm = cm_ref.shape[1]
    for p in range(2):
        fo = (_dot(cm_ref[p], rc_scr[p]) - _dot(sm_ref[p], rs_scr[p])) * scale
        for c in range(F_W // 128):
            fo_scr[c, pl.ds(p, tm, stride=2), :] = fo[:, c * 128:(c + 1) * 128]
    fo = jnp.concatenate([fo_scr[c] for c in range(F_W // 128)], axis=-1).astype(bf16)
    mix = _dot(o_ref[0], wout_ref[0:V_W, :]) + _dot(fo, wout_ref[V_W:, :])
    y_ref[0] = x_ref[0] + mod_ref[0, 0, 5:6, :] * mix


def _mixout_call(x, mods, mrow, l, o, xc, xs, cm, sm, w_out):
    b, s, d = x.shape
    half = s // 2
    tm = min(OUT_TM, half)
    row = lambda i, j: (i, j, 0)
    seq = lambda i, j: (i, 0, 0)
    return pl.pallas_call(
        functools.partial(_mixout_kernel, scale=1.0 / math.sqrt(F_GROUP * s), half=half),
        grid=(b, half // tm),
        in_specs=[
            pl.BlockSpec((1, 2 * tm, d), row),
            _mod_spec(l, mrow),
            pl.BlockSpec((1, 2 * tm, V_W), row),
            pl.BlockSpec((1, s, F_W), seq),
            pl.BlockSpec((1, s, F_W), seq),
            pl.BlockSpec((2, tm, half), lambda i, j: (0, j, 0)),
            pl.BlockSpec((2, tm, half), lambda i, j: (0, j, 0)),
            _layer(w_out.shape[1:], l),
        ],
        out_specs=pl.BlockSpec((1, 2 * tm, d), row),
        out_shape=jax.ShapeDtypeStruct(x.shape, f32),
        scratch_shapes=[pltpu.VMEM((2, half, F_W), bf16), pltpu.VMEM((2, half, F_W), bf16),
                        pltpu.VMEM((F_W // 128, 2 * tm, 128), f32)],
        compiler_params=_params(("parallel", "arbitrary")),
        name="mix_out",
    )(x, mods, o, xc, xs, cm, sm, w_out)


def _dft_kernel(ac_ref, as_ref, bc_ref, bs_ref, cm_ref, sm_ref):
    bc, bsn = bc_ref[0], bs_ref[0]
    for i in range(DFT_A_PER_STEP):
        ac, asn = ac_ref[i], as_ref[i]
        cm_ref[0, i * DFT_B:(i + 1) * DFT_B, :] = (ac * bc - asn * bsn).astype(bf16)
        sm_ref[0, i * DFT_B:(i + 1) * DFT_B, :] = (asn * bc + ac * bsn).astype(bf16)


def _dft_tables(s):
    half = s // 2
    pos = np.arange(half, dtype=np.int64)
    ang = lambda freq: 2.0 * np.pi * ((freq[..., None] * pos) % s).astype(np.float64) / s
    a = ang(np.arange(half // DFT_B, dtype=np.int64) * (2 * DFT_B))[:, None, :]
    b = ang(2 * np.arange(DFT_B, dtype=np.int64)[None, :] + np.arange(2, dtype=np.int64)[:, None])
    tab = lambda m: jnp.asarray(m.astype(np.float32))
    return tab(np.cos(a)), tab(np.sin(a)), tab(np.cos(b)), tab(np.sin(b))


def _dft_call(s):
    half = s // 2
    rows = DFT_B * DFT_A_PER_STEP
    assert half % rows == 0
    ac, asn, bc, bsn = _dft_tables(s)
    aspec = pl.BlockSpec((DFT_A_PER_STEP, 1, half), lambda p, a: (a, 0, 0))
    bspec = pl.BlockSpec((1, DFT_B, half), lambda p, a: (p, 0, 0))
    out = pl.BlockSpec((1, rows, half), lambda p, a: (p, a, 0))
    return pl.pallas_call(
        _dft_kernel,
        grid=(2, half // rows),
        in_specs=[aspec, aspec, bspec, bspec],
        out_specs=[out, out],
        out_shape=[jax.ShapeDtypeStruct((2, half, half), bf16)] * 2,
        compiler_params=_params(("parallel", "parallel")),
        name="dft_matrix",
    )(ac, asn, bc, bsn)


def _rel_bucket_np(rel):
    nb = NUM_BUCKETS // 2
    max_exact = nb // 2
    ret = (rel > 0).astype(np.int32) * nb
    n = np.abs(rel)
    nf = np.maximum(n, 1).astype(np.float32)
    ratio = np.log(nf / np.float32(max_exact)) / np.float32(math.log(REL_MAX_DISTANCE / max_exact))
    large = max_exact + (ratio * np.float32(nb - max_exact)).astype(np.int32)
    large = np.minimum(large, nb - 1)
    return (ret + np.where(n < max_exact, n, large)).astype(np.int32)


def _bucket_tiles():
    qpos = np.arange(ATT_TQ)[:, None]
    kpos = np.arange(ATT_TK)[None, :]
    return np.stack([_rel_bucket_np((d * ATT_TK + kpos) - qpos) for d in range(-2, N_BIAS_TILES - 2)])


def _bias_kernel(table_ref, bucket_ref, o_ref):
    n_maps = o_ref.shape[0]

    def rows(r, carry):
        r0 = pl.multiple_of(r * BIAS_ROWS, BIAS_ROWS)
        bucket = bucket_ref[0, pl.ds(r0, BIAS_ROWS), :]
        accs = [jnp.zeros(bucket.shape, f32) for _ in range(n_maps)]
        for bkt in range(NUM_BUCKETS):
            hit = bucket == bkt
            accs = [jnp.where(hit, table_ref[bkt, hj], acc) for hj, acc in enumerate(accs)]
        for hj, acc in enumerate(accs):
            o_ref[hj, 0, pl.ds(r0, BIAS_ROWS), :] = acc * LOG2E
        return carry

    lax.fori_loop(0, ATT_TQ // BIAS_ROWS, rows, 0)


def _bias_call(rel_bias):
    n_maps = 2 * N_HEADS
    return pl.pallas_call(
        _bias_kernel,
        grid=(N_BIAS_TILES,),
        in_specs=[
            pl.BlockSpec(memory_space=pltpu.SMEM),
            pl.BlockSpec((1, ATT_TQ, ATT_TK), lambda t: (t, 0, 0)),
        ],
        out_specs=pl.BlockSpec((n_maps, 1, ATT_TQ, ATT_TK), lambda t: (0, t, 0, 0)),
        out_shape=jax.ShapeDtypeStruct((n_maps, N_BIAS_TILES, ATT_TQ, ATT_TK), f32),
        compiler_params=_params(("arbitrary",)),
        name="rel_bias_tiles",
    )(rel_bias.reshape(NUM_BUCKETS, n_maps), jnp.asarray(_bucket_tiles()))


def _channel_dft_tables():
    idx = np.arange(F_GROUP)
    ang = 2.0 * np.pi * ((idx[:, None] * idx[None, :]) % F_GROUP) / F_GROUP
    eye = np.eye(256 // F_GROUP)
    return (jnp.asarray(np.kron(eye, np.cos(ang)), dtype=bf16), jnp.asarray(np.kron(eye, np.sin(ang)), dtype=bf16))


def _group_mean_table():
    return jnp.asarray(np.kron(np.eye(256 // HEAD_DIM), np.full((HEAD_DIM, HEAD_DIM), 1.0 / HEAD_DIM)), dtype=bf16)


def _trunk(x, mods, mrow, p, bias, cm, sm, consts):
    gmean, dc, ds = consts
    depth = p["w_in"].shape[0]
    for l in range(depth):
        lambda_init = 0.8 - 0.6 * math.exp(-0.3 * l)
        x = _ffn_call(x, mods, mrow, l, p["norm_ffn1"], p["ffn1_wi"], p["ffn1_wo"], 0)
        q, kt, vx, xc, xs = _mixin_call(x, mods, mrow, l, p["norm_mix"], p["w_in"], p["q_norm"], p["k_norm"],
                                        gmean, dc, ds)
        o = _attn_call(l, p["lambda_qk"], q, kt, vx, bias, p["subln"], lambda_init)
        x = _mixout_call(x, mods, mrow, l, o, xc, xs, cm, sm, p["w_out"])
        x = _ffn_call(x, mods, mrow, l, p["norm_ffn2"], p["ffn2_wi"], p["ffn2_wo"], 6)
    return x


def kernel(x_prompt, x_sample, c_prompt, c_sample, ada_w, ada_b, norm_ffn1, norm_mix, norm_ffn2, ffn1_wi, ffn1_wo,
           ffn2_wi, ffn2_wo, w_in, w_out, q_norm, k_norm, lambda_qk, subln, rel_bias):
    depth = ada_w.shape[0]
    nb_p, nb_s = c_prompt.shape[0], c_sample.shape[0]
    assert nb_p + nb_s <= MOD_ROWS
    c_all = jnp.concatenate([c_prompt, c_sample, jnp.zeros((MOD_ROWS - nb_p - nb_s, D_MODEL), f32)], axis=0)
    mods = _mod_call(c_all, ada_w, ada_b).reshape(depth, MOD_ROWS, N_MOD, D_MODEL)

    row = lambda a: a.reshape(depth, 1, -1)
    tile_heads = lambda a: jnp.tile(a, (1, QK_W // HEAD_DIM)).reshape(depth, 1, QK_W)
    p = dict(
        norm_ffn1=row(norm_ffn1), norm_mix=row(norm_mix), norm_ffn2=row(norm_ffn2),
        ffn1_wi=ffn1_wi.astype(bf16), ffn1_wo=ffn1_wo.astype(bf16),
        ffn2_wi=ffn2_wi.astype(bf16), ffn2_wo=ffn2_wo.astype(bf16),
        w_in=w_in.astype(bf16), w_out=w_out.astype(bf16),
        q_norm=tile_heads(q_norm), k_norm=tile_heads(k_norm),
        lambda_qk=lambda_qk, subln=row(subln),
    )
    bias = _bias_call(rel_bias)
    consts = (_group_mean_table(),) + _channel_dft_tables()

    outs = []
    for x, mrow in ((x_prompt, 0), (x_sample, nb_p)):
        cm, sm = _dft_call(x.shape[1])
        outs.append(_trunk(x, mods, mrow, p, bias, cm, sm, consts))
    return tuple(outs)
```

```python
import functools
import math

import numpy as np
import jax
import jax.numpy as jnp
from jax import lax
from jax.experimental import pallas as pl
from jax.experimental.pallas import tpu as pltpu

D_MODEL = 1024
HEAD_DIM = 64
N_HEADS = 4
HEAD_W = 2 * HEAD_DIM
QK_W = N_HEADS * HEAD_W
V_W = QK_W
F_W = 512
F_GROUP = 64
D_FF = 2816
N_MOD = 9
NUM_BUCKETS = 32
REL_MAX_DISTANCE = 128
EPS = 1e-6
ATTN_SCALE = HEAD_DIM ** -0.5

V7X_VMEM_LIMIT_BYTES = 56 * 1024 * 1024
MOD_ROWS = 16
MOD_TN = 1024
FFN_TM = 512
FFN_CK = 256
MIX_TM = 512
ATT_TQ = 512
ATT_TK = 256
ATT_UNROLL = 8
N_BIAS_TILES = ATT_TQ // ATT_TK + 4
BIAS_ROWS = 16
LOG2E = math.log2(math.e)
OUT_TM = 256
DFT_B = 64
DFT_A_PER_STEP = 4

f32 = jnp.float32
bf16 = jnp.bfloat16


def _dot(a, b):
    return jnp.dot(a, b, preferred_element_type=f32)


def _params(sem, vmem=V7X_VMEM_LIMIT_BYTES):
    return pltpu.CompilerParams(dimension_semantics=sem, vmem_limit_bytes=vmem)


def _resident(shape):
    return pl.BlockSpec(shape, lambda *_: (0,) * len(shape), pipeline_mode=pl.Buffered(1))


def _layer(shape, l):
    return pl.BlockSpec((None,) + tuple(shape), lambda *_: (l,) + (0,) * len(shape), pipeline_mode=pl.Buffered(1))


def _split_bf16(a):
    hi = a.astype(bf16)
    lo = (a - hi.astype(f32)).astype(bf16)
    return hi, lo


def _mod_norm(x, gain, shift, scale):
    ms = jnp.mean(x * x, axis=-1, keepdims=True)
    return (x * lax.rsqrt(ms + EPS) * gain) * (1.0 + scale) + shift


def _mod_kernel(c_ref, w_ref, b_ref, o_ref):
    c = c_ref[...]
    sc = c * (1.0 / (1.0 + jnp.exp(-c)))
    s_hi, s_lo = _split_bf16(sc)
    w_hi, w_lo = _split_bf16(w_ref[0])
    o_ref[0] = _dot(s_hi, w_hi) + _dot(s_hi, w_lo) + _dot(s_lo, w_hi) + b_ref[0]


def _mod_call(c_all, ada_w, ada_b):
    depth, _, n = ada_w.shape
    return pl.pallas_call(
        _mod_kernel,
        grid=(depth, n // MOD_TN),
        in_specs=[
            pl.BlockSpec((MOD_ROWS, D_MODEL), lambda l, j: (0, 0)),
            pl.BlockSpec((1, D_MODEL, MOD_TN), lambda l, j: (l, 0, j)),
            pl.BlockSpec((1, 1, MOD_TN), lambda l, j: (l, 0, j)),
        ],
        out_specs=pl.BlockSpec((1, MOD_ROWS, MOD_TN), lambda l, j: (l, 0, j)),
        out_shape=jax.ShapeDtypeStruct((depth, MOD_ROWS, n), f32),
        compiler_params=_params(("arbitrary", "arbitrary")),
        name="adaln_mod",
    )(c_all, ada_w, ada_b.reshape(depth, 1, n))


def _ffn_kernel(x_ref, mod_ref, g_ref, wi_ref, wo_ref, o_ref, acc_ref, *, row0):
    x = x_ref[0]
    shift, scale, gate = (mod_ref[0, 0, row0 + i:row0 + i + 1, :] for i in range(3))
    hb = _mod_norm(x, g_ref[...], shift, scale).astype(bf16)
    for c in range(D_FF // FFN_CK):
        g = _dot(hb, wi_ref[:, c * FFN_CK:(c + 1) * FFN_CK])
        u = _dot(hb, wi_ref[:, D_FF + c * FFN_CK:D_FF + (c + 1) * FFN_CK])
        a = (g * (1.0 / (1.0 + jnp.exp(-g))) * u).astype(bf16)
        part = _dot(a, wo_ref[c * FFN_CK:(c + 1) * FFN_CK, :])
        if c == 0:
            acc_ref[...] = part
        else:
            acc_ref[...] += part
    o_ref[0] = x + (0.5 * gate) * acc_ref[...]


def _mod_spec(l, row0):
    return pl.BlockSpec((1, 1, N_MOD, D_MODEL), lambda i, *_: (l, row0 + i, 0, 0))


def _ffn_call(x, mods, mrow, l, gain, wi, wo, row0):
    b, s, d = x.shape
    tm = min(FFN_TM, s)
    return pl.pallas_call(
        functools.partial(_ffn_kernel, row0=row0),
        grid=(b, s // tm),
        in_specs=[
            pl.BlockSpec((1, tm, d), lambda i, j: (i, j, 0)),
            _mod_spec(l, mrow),
            _layer((1, d), l),
            _layer((d, 2 * D_FF), l),
            _layer((D_FF, d), l),
        ],
        out_specs=pl.BlockSpec((1, tm, d), lambda i, j: (i, j, 0)),
        out_shape=jax.ShapeDtypeStruct(x.shape, f32),
        scratch_shapes=[pltpu.VMEM((tm, d), f32)],
        compiler_params=_params(("parallel", "parallel")),
        name="ffn",
    )(x, mods, gain, wi, wo)


def _group_rms(z, gmean_ref, gain):
    outs = []
    for c in range(QK_W // 256):
        zc = z[:, c * 256:(c + 1) * 256]
        ms = _dot((zc * zc).astype(bf16), gmean_ref[...])
        outs.append(zc * lax.rsqrt(ms + EPS))
    return jnp.concatenate(outs, axis=-1) * gain


def _mixin_kernel(x_ref, mod_ref, g_ref, win_ref, gq_ref, gk_ref, gmean_ref, dc_ref, ds_ref,
                  q_ref, kt_ref, v_ref, xc_ref, xs_ref):
    x = x_ref[0]
    shift, scale = mod_ref[0, 0, 3:4, :], mod_ref[0, 0, 4:5, :]
    hb = _mod_norm(x, g_ref[...], shift, scale).astype(bf16)
    zq = _dot(hb, win_ref[:, 0:QK_W])
    q = (_group_rms(zq, gmean_ref, gq_ref[...]) * (ATTN_SCALE * LOG2E)).astype(bf16)
    first = lax.broadcasted_iota(jnp.int32, q.shape, 1) % HEAD_W < HEAD_DIM
    zero = jnp.zeros_like(q)
    q_ref[0, 0] = jnp.where(first, q, zero)
    q_ref[1, 0] = jnp.where(first, zero, q)
    zk = _dot(hb, win_ref[:, QK_W:2 * QK_W])
    kt_ref[0] = _group_rms(zk, gmean_ref, gk_ref[...]).T.astype(bf16)
    v = _dot(hb, win_ref[:, 2 * QK_W:2 * QK_W + V_W]).astype(bf16)
    ones = jnp.ones((v.shape[0], HEAD_W), bf16)
    for h in range(N_HEADS):
        v_ref[0, :, 2 * h * HEAD_W:(2 * h + 1) * HEAD_W] = v[:, h * HEAD_W:(h + 1) * HEAD_W]
        v_ref[0, :, (2 * h + 1) * HEAD_W:(2 * h + 2) * HEAD_W] = ones
    fb = _dot(hb, win_ref[:, 2 * QK_W + V_W:]).astype(bf16)
    for c in range(F_W // 256):
        fc = fb[:, c * 256:(c + 1) * 256]
        xc_ref[0, :, c * 256:(c + 1) * 256] = _dot(fc, dc_ref[...]).astype(bf16)
        xs_ref[0, :, c * 256:(c + 1) * 256] = _dot(fc, ds_ref[...]).astype(bf16)


def _mixin_call(x, mods, mrow, l, gain, w_in, gq, gk, gmean, dc, ds):
    b, s, d = x.shape
    tm = min(MIX_TM, s)
    row = lambda i, j: (i, j, 0)
    tok = lambda w: jax.ShapeDtypeStruct((b, s, w), bf16)
    return pl.pallas_call(
        _mixin_kernel,
        grid=(b, s // tm),
        in_specs=[
            pl.BlockSpec((1, tm, d), row),
            _mod_spec(l, mrow),
            _layer((1, d), l),
            _layer(w_in.shape[1:], l),
            _layer((1, QK_W), l),
            _layer((1, QK_W), l),
            _resident((256, 256)),
            _resident((256, 256)),
            _resident((256, 256)),
        ],
        out_specs=[
            pl.BlockSpec((2, 1, tm, QK_W), lambda i, j: (0, i, j, 0)),
            pl.BlockSpec((1, QK_W, tm), lambda i, j: (i, 0, j)),
            pl.BlockSpec((1, tm, 2 * V_W), row),
            pl.BlockSpec((1, tm, F_W), row),
            pl.BlockSpec((1, tm, F_W), row),
        ],
        out_shape=[jax.ShapeDtypeStruct((2, b, s, QK_W), bf16), jax.ShapeDtypeStruct((b, QK_W, s), bf16),
                   tok(2 * V_W), tok(F_W), tok(F_W)],
        compiler_params=_params(("parallel", "parallel")),
        name="mix_in",
    )(x, mods, gain, w_in, gq, gk, gmean, dc, ds)


def _attn_kernel(lqk_ref, q_ref, kt_ref, v_ref, bias_ref, o_ref, s_scr, m_scr, *, n_tiles, lambda_init):
    tq, tk = ATT_TQ, ATT_TK
    qt = pl.program_id(2)

    def score_map(j, carry):
        qj = q_ref[j, 0]
        m_run = jnp.full((tq, 128), -jnp.inf, bf16)
        for kt in range(n_tiles):
            tile = jnp.clip(kt - qt * (tq // tk) + 2, 0, N_BIAS_TILES - 1)
            sj = _dot(qj, kt_ref[0, :, kt * tk:(kt + 1) * tk]) + bias_ref[j, tile]
            s_scr[j, :, kt * tk:(kt + 1) * tk] = sj
            m_run = jnp.maximum(m_run, jnp.maximum(sj[:, :128], sj[:, 128:]).astype(bf16))
        m_scr[j] = m_run.astype(f32)
        return carry

    lax.fori_loop(0, 2, score_map, 0)

    heads = []
    for j in range(2):
        m = jnp.max(m_scr[j], axis=-1, keepdims=True)
        p = jnp.exp2((s_scr[j] - m).astype(bf16))
        ov = _dot(p, v_ref[0])
        heads.append(ov[:, :HEAD_W] / ov[:, HEAD_W:])

    lqk = lqk_ref[...]
    lam = (jnp.exp(jnp.sum(lqk[0:1] * lqk[1:2], axis=-1, keepdims=True))
           - jnp.exp(jnp.sum(lqk[2:3] * lqk[3:4], axis=-1, keepdims=True)) + lambda_init)
    o_ref[0] = (heads[0] - lam * heads[1]).astype(bf16)


def _attn_call(l, lqk, q, kt, vx, bias, lambda_init):
    _, b, s, _ = q.shape
    tq, tk = ATT_TQ, ATT_TK
    assert s % tk == 0 and s % tq == 0
    return pl.pallas_call(
        functools.partial(_attn_kernel, n_tiles=s // tk, lambda_init=lambda_init),
        grid=(b, N_HEADS, s // tq),
        in_specs=[
            _layer((4, HEAD_DIM), l),
            pl.BlockSpec((2, 1, tq, HEAD_W), lambda i, h, j: (0, i, j, h)),
            pl.BlockSpec((1, HEAD_W, s), lambda i, h, j: (i, h, 0)),
            pl.BlockSpec((1, s, 2 * HEAD_W), lambda i, h, j: (i, 0, h)),
            pl.BlockSpec((2, N_BIAS_TILES, tq, tk), lambda i, h, j: (h, 0, 0, 0)),
        ],
        out_specs=pl.BlockSpec((1, tq, HEAD_W), lambda i, h, j: (i, j, h)),
        out_shape=jax.ShapeDtypeStruct((b, s, V_W), bf16),
        scratch_shapes=[pltpu.VMEM((2, tq, s), f32), pltpu.VMEM((2, tq, 128), f32)],
        compiler_params=_params(("parallel", "parallel", "arbitrary")),
        name="diff_attn",
    )(lqk, q, kt, vx, bias)


def _mixout_kernel(x_ref, mod_ref, o_ref, g_ref, xc_ref, xs_ref, cm_ref, sm_ref, wout_ref, y_ref,
                   rc_scr, rs_scr, fo_scr, *, scale, half, lambda_init):
    @pl.when(pl.program_id(1) == 0)
    def _():
        for src, dst in ((xc_ref, rc_scr), (xs_ref, rs_scr)):
            lo, hi = src[0, :half].astype(f32), src[0, half:].astype(f32)
            dst[0] = (lo + hi).astype(bf16)
            dst[1] = (lo - hi).astype(bf16)

    tm = cm_ref.shape[1]
    for p in range(2):
        fo = (_dot(cm_ref[p], rc_scr[p]) - _dot(sm_ref[p], rs_scr[p])) * scale
        for c in range(F_W // 128):
            fo_scr[c, pl.ds(p, tm, stride=2), :] = fo[:, c * 128:(c + 1) * 128]
    fo = jnp.concatenate([fo_scr[c] for c in range(F_W // 128)], axis=-1).astype(bf16)
    heads = []
    for h in range(N_HEADS):
        oh = o_ref[0, :, h * HEAD_W:(h + 1) * HEAD_W].astype(f32)
        ms = jnp.mean(oh * oh, axis=-1, keepdims=True)
        heads.append(((oh * lax.rsqrt(ms + EPS) * g_ref[...]) * (1.0 - lambda_init)).astype(bf16))
    mix = _dot(jnp.concatenate(heads, axis=-1), wout_ref[0:V_W, :]) + _dot(fo, wout_ref[V_W:, :])
    y_ref[0] = x_ref[0] + mod_ref[0, 0, 5:6, :] * mix


def _mixout_call(x, mods, mrow, l, o, subln, xc, xs, cm, sm, w_out, lambda_init):
    b, s, d = x.shape
    half = s // 2
    tm = min(OUT_TM, half)
    row = lambda i, j: (i, j, 0)
    seq = lambda i, j: (i, 0, 0)
    return pl.pallas_call(
        functools.partial(_mixout_kernel, scale=1.0 / math.sqrt(F_GROUP * s), half=half, lambda_init=lambda_init),
        grid=(b, half // tm),
        in_specs=[
            pl.BlockSpec((1, 2 * tm, d), row),
            _mod_spec(l, mrow),
            pl.BlockSpec((1, 2 * tm, V_W), row),
            _layer((1, HEAD_W), l),
            pl.BlockSpec((1, s, F_W), seq),
            pl.BlockSpec((1, s, F_W), seq),
            pl.BlockSpec((2, tm, half), lambda i, j: (0, j, 0)),
            pl.BlockSpec((2, tm, half), lambda i, j: (0, j, 0)),
            _layer(w_out.shape[1:], l),
        ],
        out_specs=pl.BlockSpec((1, 2 * tm, d), row),
        out_shape=jax.ShapeDtypeStruct(x.shape, f32),
        scratch_shapes=[pltpu.VMEM((2, half, F_W), bf16), pltpu.VMEM((2, half, F_W), bf16),
                        pltpu.VMEM((F_W // 128, 2 * tm, 128), f32)],
        compiler_params=_params(("parallel", "arbitrary")),
        name="mix_out",
    )(x, mods, o, subln, xc, xs, cm, sm, w_out)


def _dft_kernel(ac_ref, as_ref, bc_ref, bs_ref, cm_ref, sm_ref):
    bc, bsn = bc_ref[0], bs_ref[0]
    for i in range(DFT_A_PER_STEP):
        ac, asn = ac_ref[i], as_ref[i]
        cm_ref[0, i * DFT_B:(i + 1) * DFT_B, :] = (ac * bc - asn * bsn).astype(bf16)
        sm_ref[0, i * DFT_B:(i + 1) * DFT_B, :] = (asn * bc + ac * bsn).astype(bf16)


def _dft_tables(s):
    half = s // 2
    pos = np.arange(half, dtype=np.int64)
    ang = lambda freq: 2.0 * np.pi * ((freq[..., None] * pos) % s).astype(np.float64) / s
    a = ang(np.arange(half // DFT_B, dtype=np.int64) * (2 * DFT_B))[:, None, :]
    b = ang(2 * np.arange(DFT_B, dtype=np.int64)[None, :] + np.arange(2, dtype=np.int64)[:, None])
    tab = lambda m: jnp.asarray(m.astype(np.float32))
    return tab(np.cos(a)), tab(np.sin(a)), tab(np.cos(b)), tab(np.sin(b))


def _dft_call(s):
    half = s // 2
    rows = DFT_B * DFT_A_PER_STEP
    assert half % rows == 0
    ac, asn, bc, bsn = _dft_tables(s)
    aspec = pl.BlockSpec((DFT_A_PER_STEP, 1, half), lambda p, a: (a, 0, 0))
    bspec = pl.BlockSpec((1, DFT_B, half), lambda p, a: (p, 0, 0))
    out = pl.BlockSpec((1, rows, half), lambda p, a: (p, a, 0))
    return pl.pallas_call(
        _dft_kernel,
        grid=(2, half // rows),
        in_specs=[aspec, aspec, bspec, bspec],
        out_specs=[out, out],
        out_shape=[jax.ShapeDtypeStruct((2, half, half), bf16)] * 2,
        compiler_params=_params(("parallel", "parallel")),
        name="dft_matrix",
    )(ac, asn, bc, bsn)


def _rel_bucket_np(rel):
    nb = NUM_BUCKETS // 2
    max_exact = nb // 2
    ret = (rel > 0).astype(np.int32) * nb
    n = np.abs(rel)
    nf = np.maximum(n, 1).astype(np.float32)
    ratio = np.log(nf / np.float32(max_exact)) / np.float32(math.log(REL_MAX_DISTANCE / max_exact))
    large = max_exact + (ratio * np.float32(nb - max_exact)).astype(np.int32)
    large = np.minimum(large, nb - 1)
    return (ret + np.where(n < max_exact, n, large)).astype(np.int32)


def _bucket_tiles():
    qpos = np.arange(ATT_TQ)[:, None]
    kpos = np.arange(ATT_TK)[None, :]
    return np.stack([_rel_bucket_np((d * ATT_TK + kpos) - qpos) for d in range(-2, N_BIAS_TILES - 2)])


def _bias_kernel(table_ref, bucket_ref, o_ref):
    n_maps = o_ref.shape[0]

    def rows(r, carry):
        r0 = pl.multiple_of(r * BIAS_ROWS, BIAS_ROWS)
        bucket = bucket_ref[0, pl.ds(r0, BIAS_ROWS), :]
        accs = [jnp.zeros(bucket.shape, f32) for _ in range(n_maps)]
        for bkt in range(NUM_BUCKETS):
            hit = bucket == bkt
            accs = [jnp.where(hit, table_ref[bkt, hj], acc) for hj, acc in enumerate(accs)]
        for hj, acc in enumerate(accs):
            o_ref[hj, 0, pl.ds(r0, BIAS_ROWS), :] = acc * LOG2E
        return carry

    lax.fori_loop(0, ATT_TQ // BIAS_ROWS, rows, 0)


def _bias_call(rel_bias):
    n_maps = 2 * N_HEADS
    return pl.pallas_call(
        _bias_kernel,
        grid=(N_BIAS_TILES,),
        in_specs=[
            pl.BlockSpec(memory_space=pltpu.SMEM),
            pl.BlockSpec((1, ATT_TQ, ATT_TK), lambda t: (t, 0, 0)),
        ],
        out_specs=pl.BlockSpec((n_maps, 1, ATT_TQ, ATT_TK), lambda t: (0, t, 0, 0)),
        out_shape=jax.ShapeDtypeStruct((n_maps, N_BIAS_TILES, ATT_TQ, ATT_TK), f32),
        compiler_params=_params(("arbitrary",)),
        name="rel_bias_tiles",
    )(rel_bias.reshape(NUM_BUCKETS, n_maps), jnp.asarray(_bucket_tiles()))


def _channel_dft_tables():
    idx = np.arange(F_GROUP)
    ang = 2.0 * np.pi * ((idx[:, None] * idx[None, :]) % F_GROUP) / F_GROUP
    eye = np.eye(256 // F_GROUP)
    return (jnp.asarray(np.kron(eye, np.cos(ang)), dtype=bf16), jnp.asarray(np.kron(eye, np.sin(ang)), dtype=bf16))


def _group_mean_table():
    return jnp.asarray(np.kron(np.eye(256 // HEAD_DIM), np.full((HEAD_DIM, HEAD_DIM), 1.0 / HEAD_DIM)), dtype=bf16)


def _trunk(x, mods, mrow, p, bias, cm, sm, consts):
    gmean, dc, ds = consts
    depth = p["w_in"].shape[0]
    for l in range(depth):
        lambda_init = 0.8 - 0.6 * math.exp(-0.3 * l)
        x = _ffn_call(x, mods, mrow, l, p["norm_ffn1"], p["ffn1_wi"], p["ffn1_wo"], 0)
        q, kt, vx, xc, xs = _mixin_call(x, mods, mrow, l, p["norm_mix"], p["w_in"], p["q_norm"], p["k_norm"],
                                        gmean, dc, ds)
        o = _attn_call(l, p["lambda_qk"], q, kt, vx, bias, lambda_init)
        x = _mixout_call(x, mods, mrow, l, o, p["subln"], xc, xs, cm, sm, p["w_out"], lambda_init)
        x = _ffn_call(x, mods, mrow, l, p["norm_ffn2"], p["ffn2_wi"], p["ffn2_wo"], 6)
    return x


def kernel(x_prompt, x_sample, c_prompt, c_sample, ada_w, ada_b, norm_ffn1, norm_mix, norm_ffn2, ffn1_wi, ffn1_wo,
           ffn2_wi, ffn2_wo, w_in, w_out, q_norm, k_norm, lambda_qk, subln, rel_bias):
    depth = ada_w.shape[0]
    nb_p, nb_s = c_prompt.shape[0], c_sample.shape[0]
    assert nb_p + nb_s <= MOD_ROWS
    c_all = jnp.concatenate([c_prompt, c_sample, jnp.zeros((MOD_ROWS - nb_p - nb_s, D_MODEL), f32)], axis=0)
    mods = _mod_call(c_all, ada_w, ada_b).reshape(depth, MOD_ROWS, N_MOD, D_MODEL)

    row = lambda a: a.reshape(depth, 1, -1)
    tile_heads = lambda a: jnp.tile(a, (1, QK_W // HEAD_DIM)).reshape(depth, 1, QK_W)
    p = dict(
        norm_ffn1=row(norm_ffn1), norm_mix=row(norm_mix), norm_ffn2=row(norm_ffn2),
        ffn1_wi=ffn1_wi.astype(bf16), ffn1_wo=ffn1_wo.astype(bf16),
        ffn2_wi=ffn2_wi.astype(bf16), ffn2_wo=ffn2_wo.astype(bf16),
        w_in=w_in.astype(bf16), w_out=w_out.astype(bf16),
        q_norm=tile_heads(q_norm), k_norm=tile_heads(k_norm),
        lambda_qk=lambda_qk, subln=row(subln),
    )
    bias = _bias_call(rel_bias)
    consts = (_group_mean_table(),) + _channel_dft_tables()

    outs = []
    for x, mrow in ((x_prompt, 0), (x_sample, nb_p)):
        cm, sm = _dft_call(x.shape[1])
        outs.append(_trunk(x, mods, mrow, p, bias, cm, sm, consts))
    return tuple(outs)
```

```python
import functools
import math

import numpy as np
import jax
import jax.numpy as jnp
from jax import lax
from jax.experimental import pallas as pl
from jax.experimental.pallas import tpu as pltpu

D_MODEL = 1024
HEAD_DIM = 64
N_HEADS = 4
HEAD_W = 2 * HEAD_DIM
QK_W = N_HEADS * HEAD_W
V_W = QK_W
F_W = 512
F_GROUP = 64
D_FF = 2816
N_MOD = 9
NUM_BUCKETS = 32
REL_MAX_DISTANCE = 128
EPS = 1e-6
ATTN_SCALE = HEAD_DIM ** -0.5

V7X_VMEM_LIMIT_BYTES = 56 * 1024 * 1024
MOD_ROWS = 16
MOD_TN = 1024
FFN_TM = 1024
FFN_SUB = 512
FFN_CK = 256
MIX_TM = 512
ATT_TQ = 512
ATT_TK = 256
ATT_UNROLL = 8
N_BIAS_TILES = ATT_TQ // ATT_TK + 4
BIAS_ROWS = 128
LOG2E = math.log2(math.e)
OUT_TM = 256
DFT_B = 64
DFT_A_PER_STEP = 4

f32 = jnp.float32
bf16 = jnp.bfloat16


def _dot(a, b):
    return jnp.dot(a, b, preferred_element_type=f32)


def _params(sem, vmem=V7X_VMEM_LIMIT_BYTES):
    return pltpu.CompilerParams(dimension_semantics=sem, vmem_limit_bytes=vmem)


def _resident(shape):
    return pl.BlockSpec(shape, lambda *_: (0,) * len(shape), pipeline_mode=pl.Buffered(1))


def _layer(shape, l):
    return pl.BlockSpec((None,) + tuple(shape), lambda *_: (l,) + (0,) * len(shape), pipeline_mode=pl.Buffered(1))


def _split_bf16(a):
    hi = a.astype(bf16)
    lo = (a - hi.astype(f32)).astype(bf16)
    return hi, lo


def _mod_norm(x, gain, shift, scale):
    ms = jnp.mean(x * x, axis=-1, keepdims=True)
    return (x * lax.rsqrt(ms + EPS) * gain) * (1.0 + scale) + shift


def _mod_kernel(c_ref, w_ref, b_ref, o_ref):
    c = c_ref[...]
    sc = c * (1.0 / (1.0 + jnp.exp(-c)))
    s_hi, s_lo = _split_bf16(sc)
    w_hi, w_lo = _split_bf16(w_ref[0])
    o_ref[0] = _dot(s_hi, w_hi) + _dot(s_hi, w_lo) + _dot(s_lo, w_hi) + b_ref[0]


def _mod_call(c_all, ada_w, ada_b):
    depth, _, n = ada_w.shape
    return pl.pallas_call(
        _mod_kernel,
        grid=(depth, n // MOD_TN),
        in_specs=[
            pl.BlockSpec((MOD_ROWS, D_MODEL), lambda l, j: (0, 0)),
            pl.BlockSpec((1, D_MODEL, MOD_TN), lambda l, j: (l, 0, j)),
            pl.BlockSpec((1, 1, MOD_TN), lambda l, j: (l, 0, j)),
        ],
        out_specs=pl.BlockSpec((1, MOD_ROWS, MOD_TN), lambda l, j: (l, 0, j)),
        out_shape=jax.ShapeDtypeStruct((depth, MOD_ROWS, n), f32),
        compiler_params=_params(("arbitrary", "arbitrary")),
        name="adaln_mod",
    )(c_all, ada_w, ada_b.reshape(depth, 1, n))


def _ffn_kernel(x_ref, mod_ref, g_ref, wi_ref, wo_ref, o_ref, acc_ref, *, row0):
    shift, scale, gate = (mod_ref[0, 0, row0 + i:row0 + i + 1, :] for i in range(3))
    subs = [slice(r, r + FFN_SUB) for r in range(0, x_ref.shape[1], FFN_SUB)]
    hbs = [_mod_norm(x_ref[0, rows, :], g_ref[...], shift, scale).astype(bf16) for rows in subs]
    for c in range(D_FF // FFN_CK):
        for rows, hb in zip(subs, hbs):
            g = _dot(hb, wi_ref[:, c * FFN_CK:(c + 1) * FFN_CK])
            u = _dot(hb, wi_ref[:, D_FF + c * FFN_CK:D_FF + (c + 1) * FFN_CK])
            a = (g * (1.0 / (1.0 + jnp.exp(-g))) * u).astype(bf16)
            part = _dot(a, wo_ref[c * FFN_CK:(c + 1) * FFN_CK, :])
            if c == 0:
                acc_ref[rows, :] = part
            else:
                acc_ref[rows, :] += part
    for rows in subs:
        o_ref[0, rows, :] = x_ref[0, rows, :] + (0.5 * gate) * acc_ref[rows, :]


def _mod_spec(l, row0):
    return pl.BlockSpec((1, 1, N_MOD, D_MODEL), lambda i, *_: (l, row0 + i, 0, 0))


def _ffn_call(x, mods, mrow, l, gain, wi, wo, row0):
    b, s, d = x.shape
    tm = min(FFN_TM, s)
    return pl.pallas_call(
        functools.partial(_ffn_kernel, row0=row0),
        grid=(b, s // tm),
        in_specs=[
            pl.BlockSpec((1, tm, d), lambda i, j: (i, j, 0)),
            _mod_spec(l, mrow),
            _layer((1, d), l),
            _layer((d, 2 * D_FF), l),
            _layer((D_FF, d), l),
        ],
        out_specs=pl.BlockSpec((1, tm, d), lambda i, j: (i, j, 0)),
        out_shape=jax.ShapeDtypeStruct(x.shape, f32),
        scratch_shapes=[pltpu.VMEM((tm, d), f32)],
        compiler_params=_params(("parallel", "parallel")),
        name="ffn",
    )(x, mods, gain, wi, wo)


def _group_rms(z, gmean_ref, gain):
    outs = []
    for c in range(QK_W // 256):
        zc = z[:, c * 256:(c + 1) * 256]
        ms = _dot((zc * zc).astype(bf16), gmean_ref[...])
        outs.append(zc * lax.rsqrt(ms + EPS))
    return jnp.concatenate(outs, axis=-1) * gain


def _mixin_kernel(x_ref, mod_ref, g_ref, win_ref, gq_ref, gk_ref, gmean_ref, dc_ref, ds_ref,
                  q_ref, kt_ref, v_ref, xc_ref, xs_ref):
    x = x_ref[0]
    shift, scale = mod_ref[0, 0, 3:4, :], mod_ref[0, 0, 4:5, :]
    hb = _mod_norm(x, g_ref[...], shift, scale).astype(bf16)
    zq = _dot(hb, win_ref[:, 0:QK_W])
    q = (_group_rms(zq, gmean_ref, gq_ref[...]) * (ATTN_SCALE * LOG2E)).astype(bf16)
    first = lax.broadcasted_iota(jnp.int32, q.shape, 1) % HEAD_W < HEAD_DIM
    zero = jnp.zeros_like(q)
    q_ref[0, 0] = jnp.where(first, q, zero)
    q_ref[1, 0] = jnp.where(first, zero, q)
    zk = _dot(hb, win_ref[:, QK_W:2 * QK_W])
    kt_ref[0] = _group_rms(zk, gmean_ref, gk_ref[...]).T.astype(bf16)
    v = _dot(hb, win_ref[:, 2 * QK_W:2 * QK_W + V_W]).astype(bf16)
    ones = jnp.ones((v.shape[0], HEAD_W), bf16)
    for h in range(N_HEADS):
        v_ref[0, :, 2 * h * HEAD_W:(2 * h + 1) * HEAD_W] = v[:, h * HEAD_W:(h + 1) * HEAD_W]
        v_ref[0, :, (2 * h + 1) * HEAD_W:(2 * h + 2) * HEAD_W] = ones
    fb = _dot(hb, win_ref[:, 2 * QK_W + V_W:]).astype(bf16)
    for c in range(F_W // 256):
        fc = fb[:, c * 256:(c + 1) * 256]
        xc_ref[0, :, c * 256:(c + 1) * 256] = _dot(fc, dc_ref[...]).astype(bf16)
        xs_ref[0, :, c * 256:(c + 1) * 256] = _dot(fc, ds_ref[...]).astype(bf16)


def _mixin_call(x, mods, mrow, l, gain, w_in, gq, gk, gmean, dc, ds):
    b, s, d = x.shape
    tm = min(MIX_TM, s)
    row = lambda i, j: (i, j, 0)
    tok = lambda w: jax.ShapeDtypeStruct((b, s, w), bf16)
    return pl.pallas_call(
        _mixin_kernel,
        grid=(b, s // tm),
        in_specs=[
            pl.BlockSpec((1, tm, d), row),
            _mod_spec(l, mrow),
            _layer((1, d), l),
            _layer(w_in.shape[1:], l),
            _layer((1, QK_W), l),
            _layer((1, QK_W), l),
            _resident((256, 256)),
            _resident((256, 256)),
            _resident((256, 256)),
        ],
        out_specs=[
            pl.BlockSpec((2, 1, tm, QK_W), lambda i, j: (0, i, j, 0)),
            pl.BlockSpec((1, QK_W, tm), lambda i, j: (i, 0, j)),
            pl.BlockSpec((1, tm, 2 * V_W), row),
            pl.BlockSpec((1, tm, F_W), row),
            pl.BlockSpec((1, tm, F_W), row),
        ],
        out_shape=[jax.ShapeDtypeStruct((2, b, s, QK_W), bf16), jax.ShapeDtypeStruct((b, QK_W, s), bf16),
                   tok(2 * V_W), tok(F_W), tok(F_W)],
        compiler_params=_params(("parallel", "parallel")),
        name="mix_in",
    )(x, mods, gain, w_in, gq, gk, gmean, dc, ds)


def _attn_kernel(lqk_ref, q_ref, kt_ref, v_ref, bias_ref, o_ref, s_scr, m_scr, *, n_tiles, lambda_init):
    tq, tk = ATT_TQ, ATT_TK
    qt = pl.program_id(2)

    def score_map(j, carry):
        qj = q_ref[j, 0]
        m_run = jnp.full((tq, 128), -jnp.inf, bf16)
        for kt in range(n_tiles):
            tile = jnp.clip(kt - qt * (tq // tk) + 2, 0, N_BIAS_TILES - 1)
            sj = _dot(qj, kt_ref[0, :, kt * tk:(kt + 1) * tk]) + bias_ref[j, tile]
            s_scr[j, :, kt * tk:(kt + 1) * tk] = sj
            m_run = jnp.maximum(m_run, jnp.maximum(sj[:, :128], sj[:, 128:]).astype(bf16))
        m_scr[j] = m_run.astype(f32)
        return carry

    lax.fori_loop(0, 2, score_map, 0)

    heads = []
    for j in range(2):
        m = jnp.max(m_scr[j], axis=-1, keepdims=True)
        p = jnp.exp2((s_scr[j] - m).astype(bf16))
        ov = _dot(p, v_ref[0])
        heads.append(ov[:, :HEAD_W] / ov[:, HEAD_W:])

    lqk = lqk_ref[...]
    lam = (jnp.exp(jnp.sum(lqk[0:1] * lqk[1:2], axis=-1, keepdims=True))
           - jnp.exp(jnp.sum(lqk[2:3] * lqk[3:4], axis=-1, keepdims=True)) + lambda_init)
    o_ref[0] = (heads[0] - lam * heads[1]).astype(bf16)


def _attn_call(l, lqk, q, kt, vx, bias, lambda_init):
    _, b, s, _ = q.shape
    tq, tk = ATT_TQ, ATT_TK
    assert s % tk == 0 and s % tq == 0
    return pl.pallas_call(
        functools.partial(_attn_kernel, n_tiles=s // tk, lambda_init=lambda_init),
        grid=(b, N_HEADS, s // tq),
        in_specs=[
            _layer((4, HEAD_DIM), l),
            pl.BlockSpec((2, 1, tq, HEAD_W), lambda i, h, j: (0, i, j, h)),
            pl.BlockSpec((1, HEAD_W, s), lambda i, h, j: (i, h, 0)),
            pl.BlockSpec((1, s, 2 * HEAD_W), lambda i, h, j: (i, 0, h)),
            pl.BlockSpec((2, N_BIAS_TILES, tq, tk), lambda i, h, j: (h, 0, 0, 0)),
        ],
        out_specs=pl.BlockSpec((1, tq, HEAD_W), lambda i, h, j: (i, j, h)),
        out_shape=jax.ShapeDtypeStruct((b, s, V_W), bf16),
        scratch_shapes=[pltpu.VMEM((2, tq, s), f32), pltpu.VMEM((2, tq, 128), f32)],
        compiler_params=_params(("parallel", "parallel", "arbitrary")),
        name="diff_attn",
    )(lqk, q, kt, vx, bias)


def _mixout_kernel(x_ref, mod_ref, o_ref, g_ref, xc_ref, xs_ref, cm_ref, sm_ref, wout_ref, y_ref,
                   rc_scr, rs_scr, fo_scr, *, scale, half, lambda_init):
    @pl.when(pl.program_id(1) == 0)
    def _():
        for src, dst in ((xc_ref, rc_scr), (xs_ref, rs_scr)):
            lo, hi = src[0, :half].astype(f32), src[0, half:].astype(f32)
            dst[0] = (lo + hi).astype(bf16)
            dst[1] = (lo - hi).astype(bf16)

    tm = cm_ref.shape[1]
    for p in range(2):
        fo = (_dot(cm_ref[p], rc_scr[p]) - _dot(sm_ref[p], rs_scr[p])) * scale
        for c in range(F_W // 128):
            fo_scr[c, pl.ds(p, tm, stride=2), :] = fo[:, c * 128:(c + 1) * 128]
    fo = jnp.concatenate([fo_scr[c] for c in range(F_W // 128)], axis=-1).astype(bf16)
    heads = []
    for h in range(N_HEADS):
        oh = o_ref[0, :, h * HEAD_W:(h + 1) * HEAD_W].astype(f32)
        ms = jnp.mean(oh * oh, axis=-1, keepdims=True)
        heads.append(((oh * lax.rsqrt(ms + EPS) * g_ref[...]) * (1.0 - lambda_init)).astype(bf16))
    mix = _dot(jnp.concatenate(heads, axis=-1), wout_ref[0:V_W, :]) + _dot(fo, wout_ref[V_W:, :])
    y_ref[0] = x_ref[0] + mod_ref[0, 0, 5:6, :] * mix


def _mixout_call(x, mods, mrow, l, o, subln, xc, xs, cm, sm, w_out, lambda_init):
    b, s, d = x.shape
    half = s // 2
    tm = min(OUT_TM, half)
    row = lambda i, j: (i, j, 0)
    seq = lambda i, j: (i, 0, 0)
    return pl.pallas_call(
        functools.partial(_mixout_kernel, scale=1.0 / math.sqrt(F_GROUP * s), half=half, lambda_init=lambda_init),
        grid=(b, half // tm),
        in_specs=[
            pl.BlockSpec((1, 2 * tm, d), row),
            _mod_spec(l, mrow),
            pl.BlockSpec((1, 2 * tm, V_W), row),
            _layer((1, HEAD_W), l),
            pl.BlockSpec((1, s, F_W), seq),
            pl.BlockSpec((1, s, F_W), seq),
            pl.BlockSpec((2, tm, half), lambda i, j: (0, j, 0)),
            pl.BlockSpec((2, tm, half), lambda i, j: (0, j, 0)),
            _layer(w_out.shape[1:], l),
        ],
        out_specs=pl.BlockSpec((1, 2 * tm, d), row),
        out_shape=jax.ShapeDtypeStruct(x.shape, f32),
        scratch_shapes=[pltpu.VMEM((2, half, F_W), bf16), pltpu.VMEM((2, half, F_W), bf16),
                        pltpu.VMEM((F_W // 128, 2 * tm, 128), f32)],
        compiler_params=_params(("parallel", "arbitrary")),
        name="mix_out",
    )(x, mods, o, subln, xc, xs, cm, sm, w_out)


def _dft_kernel(ac_ref, as_ref, bc_ref, bs_ref, cm_ref, sm_ref):
    bc, bsn = bc_ref[0], bs_ref[0]
    for i in range(DFT_A_PER_STEP):
        ac, asn = ac_ref[i], as_ref[i]
        cm_ref[0, i * DFT_B:(i + 1) * DFT_B, :] = (ac * bc - asn * bsn).astype(bf16)
        sm_ref[0, i * DFT_B:(i + 1) * DFT_B, :] = (asn * bc + ac * bsn).astype(bf16)


def _dft_tables(s):
    half = s // 2
    pos = np.arange(half, dtype=np.int64)
    ang = lambda freq: 2.0 * np.pi * ((freq[..., None] * pos) % s).astype(np.float64) / s
    a = ang(np.arange(half // DFT_B, dtype=np.int64) * (2 * DFT_B))[:, None, :]
    b = ang(2 * np.arange(DFT_B, dtype=np.int64)[None, :] + np.arange(2, dtype=np.int64)[:, None])
    tab = lambda m: jnp.asarray(m.astype(np.float32))
    return tab(np.cos(a)), tab(np.sin(a)), tab(np.cos(b)), tab(np.sin(b))


def _dft_call(s):
    half = s // 2
    rows = DFT_B * DFT_A_PER_STEP
    assert half % rows == 0
    ac, asn, bc, bsn = _dft_tables(s)
    aspec = pl.BlockSpec((DFT_A_PER_STEP, 1, half), lambda p, a: (a, 0, 0))
    bspec = pl.BlockSpec((1, DFT_B, half), lambda p, a: (p, 0, 0))
    out = pl.BlockSpec((1, rows, half), lambda p, a: (p, a, 0))
    return pl.pallas_call(
        _dft_kernel,
        grid=(2, half // rows),
        in_specs=[aspec, aspec, bspec, bspec],
        out_specs=[out, out],
        out_shape=[jax.ShapeDtypeStruct((2, half, half), bf16)] * 2,
        compiler_params=_params(("parallel", "parallel")),
        name="dft_matrix",
    )(ac, asn, bc, bsn)


def _rel_bucket_np(rel):
    nb = NUM_BUCKETS // 2
    max_exact = nb // 2
    ret = (rel > 0).astype(np.int32) * nb
    n = np.abs(rel)
    nf = np.maximum(n, 1).astype(np.float32)
    ratio = np.log(nf / np.float32(max_exact)) / np.float32(math.log(REL_MAX_DISTANCE / max_exact))
    large = max_exact + (ratio * np.float32(nb - max_exact)).astype(np.int32)
    large = np.minimum(large, nb - 1)
    return (ret + np.where(n < max_exact, n, large)).astype(np.int32)


def _bucket_tiles():
    qpos = np.arange(ATT_TQ)[:, None]
    kpos = np.arange(ATT_TK)[None, :]
    return np.stack([_rel_bucket_np((d * ATT_TK + kpos) - qpos) for d in range(-2, N_BIAS_TILES - 2)])


def _bias_kernel(table_ref, bucket_ref, o_ref):
    n_maps = o_ref.shape[0]

    def rows(r, carry):
        r0 = pl.multiple_of(r * BIAS_ROWS, BIAS_ROWS)
        for c in range(ATT_TK // 128):
            bucket = bucket_ref[0, pl.ds(r0, BIAS_ROWS), c * 128:(c + 1) * 128]
            for hj in range(n_maps):
                row = jnp.broadcast_to(table_ref[hj:hj + 1, :], bucket.shape)
                o_ref[hj, 0, pl.ds(r0, BIAS_ROWS), c * 128:(c + 1) * 128] = (
                    jnp.take_along_axis(row, bucket, axis=1) * LOG2E)
        return carry

    lax.fori_loop(0, ATT_TQ // BIAS_ROWS, rows, 0)


def _bias_call(rel_bias):
    n_maps = 2 * N_HEADS
    table = jnp.pad(rel_bias.reshape(NUM_BUCKETS, n_maps).T, ((0, 0), (0, 128 - NUM_BUCKETS)))
    return pl.pallas_call(
        _bias_kernel,
        grid=(N_BIAS_TILES,),
        in_specs=[
            _resident((n_maps, 128)),
            pl.BlockSpec((1, ATT_TQ, ATT_TK), lambda t: (t, 0, 0)),
        ],
        out_specs=pl.BlockSpec((n_maps, 1, ATT_TQ, ATT_TK), lambda t: (0, t, 0, 0)),
        out_shape=jax.ShapeDtypeStruct((n_maps, N_BIAS_TILES, ATT_TQ, ATT_TK), f32),
        compiler_params=_params(("arbitrary",)),
        name="rel_bias_tiles",
    )(table, jnp.asarray(_bucket_tiles()))


def _channel_dft_tables():
    idx = np.arange(F_GROUP)
    ang = 2.0 * np.pi * ((idx[:, None] * idx[None, :]) % F_GROUP) / F_GROUP
    eye = np.eye(256 // F_GROUP)
    return (jnp.asarray(np.kron(eye, np.cos(ang)), dtype=bf16), jnp.asarray(np.kron(eye, np.sin(ang)), dtype=bf16))


def _group_mean_table():
    return jnp.asarray(np.kron(np.eye(256 // HEAD_DIM), np.full((HEAD_DIM, HEAD_DIM), 1.0 / HEAD_DIM)), dtype=bf16)


def _trunk(x, mods, mrow, p, bias, cm, sm, consts):
    gmean, dc, ds = consts
    depth = p["w_in"].shape[0]
    for l in range(depth):
        lambda_init = 0.8 - 0.6 * math.exp(-0.3 * l)
        x = _ffn_call(x, mods, mrow, l, p["norm_ffn1"], p["ffn1_wi"], p["ffn1_wo"], 0)
        q, kt, vx, xc, xs = _mixin_call(x, mods, mrow, l, p["norm_mix"], p["w_in"], p["q_norm"], p["k_norm"],
                                        gmean, dc, ds)
        o = _attn_call(l, p["lambda_qk"], q, kt, vx, bias, lambda_init)
        x = _mixout_call(x, mods, mrow, l, o, p["subln"], xc, xs, cm, sm, p["w_out"], lambda_init)
        x = _ffn_call(x, mods, mrow, l, p["norm_ffn2"], p["ffn2_wi"], p["ffn2_wo"], 6)
    return x


def kernel(x_prompt, x_sample, c_prompt, c_sample, ada_w, ada_b, norm_ffn1, norm_mix, norm_ffn2, ffn1_wi, ffn1_wo,
           ffn2_wi, ffn2_wo, w_in, w_out, q_norm, k_norm, lambda_qk, subln, rel_bias):
    depth = ada_w.shape[0]
    nb_p, nb_s = c_prompt.shape[0], c_sample.shape[0]
    assert nb_p + nb_s <= MOD_ROWS
    c_all = jnp.concatenate([c_prompt, c_sample, jnp.zeros((MOD_ROWS - nb_p - nb_s, D_MODEL), f32)], axis=0)
    mods = _mod_call(c_all, ada_w, ada_b).reshape(depth, MOD_ROWS, N_MOD, D_MODEL)

    row = lambda a: a.reshape(depth, 1, -1)
    tile_heads = lambda a: jnp.tile(a, (1, QK_W // HEAD_DIM)).reshape(depth, 1, QK_W)
    p = dict(
        norm_ffn1=row(norm_ffn1), norm_mix=row(norm_mix), norm_ffn2=row(norm_ffn2),
        ffn1_wi=ffn1_wi.astype(bf16), ffn1_wo=ffn1_wo.astype(bf16),
        ffn2_wi=ffn2_wi.astype(bf16), ffn2_wo=ffn2_wo.astype(bf16),
        w_in=w_in.astype(bf16), w_out=w_out.astype(bf16),
        q_norm=tile_heads(q_norm), k_norm=tile_heads(k_norm),
        lambda_qk=lambda_qk, subln=row(subln),
    )
    bias = _bias_call(rel_bias)
    consts = (_group_mean_table(),) + _channel_dft_tables()

    outs = []
    for x, mrow in ((x_prompt, 0), (x_sample, nb_p)):
        cm, sm = _dft_call(x.shape[1])
        outs.append(_trunk(x, mods, mrow, p, bias, cm, sm, consts))
    return tuple(outs)
```

```python
import functools
import math

import numpy as np
import jax
import jax.numpy as jnp
from jax import lax
from jax.experimental import pallas as pl
from jax.experimental.pallas import tpu as pltpu

D_MODEL = 1024
HEAD_DIM = 64
N_HEADS = 4
HEAD_W = 2 * HEAD_DIM
QK_W = N_HEADS * HEAD_W
V_W = QK_W
F_W = 512
F_GROUP = 64
D_FF = 2816
N_MOD = 9
NUM_BUCKETS = 32
REL_MAX_DISTANCE = 128
EPS = 1e-6
ATTN_SCALE = HEAD_DIM ** -0.5

V7X_VMEM_LIMIT_BYTES = 58 * 1024 * 1024
MOD_ROWS = 16
MOD_TN = 1024
FFN_TM = 1024
FFN_SUB = 512
FFN_CK = 256
MIX_TM = 512
ATT_TQ = 512
ATT_TK = 256
ATT_UNROLL = 8
N_BIAS_TILES = ATT_TQ // ATT_TK + 4
BIAS_ROWS = 128
LOG2E = math.log2(math.e)
OUT_TM = 256
DFT_B = 64
DFT_A_PER_STEP = 4
DFT_CLASSES = (1, 3, 0, 2)

f32 = jnp.float32
bf16 = jnp.bfloat16


def _dot(a, b):
    return jnp.dot(a, b, preferred_element_type=f32)


def _params(sem, vmem=V7X_VMEM_LIMIT_BYTES):
    return pltpu.CompilerParams(dimension_semantics=sem, vmem_limit_bytes=vmem)


def _resident(shape):
    return pl.BlockSpec(shape, lambda *_: (0,) * len(shape), pipeline_mode=pl.Buffered(1))


def _layer(shape, l):
    return pl.BlockSpec((None,) + tuple(shape), lambda *_: (l,) + (0,) * len(shape), pipeline_mode=pl.Buffered(1))


def _split_bf16(a):
    hi = a.astype(bf16)
    lo = (a - hi.astype(f32)).astype(bf16)
    return hi, lo


def _mod_norm(x, gain, shift, scale):
    ms = jnp.mean(x * x, axis=-1, keepdims=True)
    return (x * lax.rsqrt(ms + EPS) * gain) * (1.0 + scale) + shift


def _mod_kernel(c_ref, w_ref, b_ref, o_ref):
    c = c_ref[...]
    sc = c * (1.0 / (1.0 + jnp.exp(-c)))
    s_hi, s_lo = _split_bf16(sc)
    w_hi, w_lo = _split_bf16(w_ref[0])
    o_ref[0] = _dot(s_hi, w_hi) + _dot(s_hi, w_lo) + _dot(s_lo, w_hi) + b_ref[0]


def _mod_call(c_all, ada_w, ada_b):
    depth, _, n = ada_w.shape
    return pl.pallas_call(
        _mod_kernel,
        grid=(depth, n // MOD_TN),
        in_specs=[
            pl.BlockSpec((MOD_ROWS, D_MODEL), lambda l, j: (0, 0)),
            pl.BlockSpec((1, D_MODEL, MOD_TN), lambda l, j: (l, 0, j)),
            pl.BlockSpec((1, 1, MOD_TN), lambda l, j: (l, 0, j)),
        ],
        out_specs=pl.BlockSpec((1, MOD_ROWS, MOD_TN), lambda l, j: (l, 0, j)),
        out_shape=jax.ShapeDtypeStruct((depth, MOD_ROWS, n), f32),
        compiler_params=_params(("arbitrary", "arbitrary")),
        name="adaln_mod",
    )(c_all, ada_w, ada_b.reshape(depth, 1, n))


def _ffn_kernel(x_ref, mod_ref, g_ref, wi_ref, wo_ref, o_ref, acc_ref, *, row0):
    shift, scale, gate = (mod_ref[0, 0, row0 + i:row0 + i + 1, :] for i in range(3))
    subs = [slice(r, r + FFN_SUB) for r in range(0, x_ref.shape[1], FFN_SUB)]
    hbs = [_mod_norm(x_ref[0, rows, :], g_ref[...], shift, scale).astype(bf16) for rows in subs]
    for c in range(D_FF // FFN_CK):
        for rows, hb in zip(subs, hbs):
            g = _dot(hb, wi_ref[:, c * FFN_CK:(c + 1) * FFN_CK])
            u = _dot(hb, wi_ref[:, D_FF + c * FFN_CK:D_FF + (c + 1) * FFN_CK])
            a = (g * (1.0 / (1.0 + jnp.exp(-g))) * u).astype(bf16)
            part = _dot(a, wo_ref[c * FFN_CK:(c + 1) * FFN_CK, :])
            if c == 0:
                acc_ref[rows, :] = part
            else:
                acc_ref[rows, :] += part
    for rows in subs:
        o_ref[0, rows, :] = x_ref[0, rows, :] + (0.5 * gate) * acc_ref[rows, :]


def _mod_spec(l, row0):
    return pl.BlockSpec((1, 1, N_MOD, D_MODEL), lambda i, *_: (l, row0 + i, 0, 0))


def _ffn_call(x, mods, mrow, l, gain, wi, wo, row0):
    b, s, d = x.shape
    tm = min(FFN_TM, s)
    return pl.pallas_call(
        functools.partial(_ffn_kernel, row0=row0),
        grid=(b, s // tm),
        in_specs=[
            pl.BlockSpec((1, tm, d), lambda i, j: (i, j, 0)),
            _mod_spec(l, mrow),
            _layer((1, d), l),
            _layer((d, 2 * D_FF), l),
            _layer((D_FF, d), l),
        ],
        out_specs=pl.BlockSpec((1, tm, d), lambda i, j: (i, j, 0)),
        out_shape=jax.ShapeDtypeStruct(x.shape, f32),
        scratch_shapes=[pltpu.VMEM((tm, d), f32)],
        compiler_params=_params(("parallel", "parallel")),
        name="ffn",
    )(x, mods, gain, wi, wo)


def _group_rms(z, gmean_ref, gain):
    outs = []
    for c in range(QK_W // 256):
        zc = z[:, c * 256:(c + 1) * 256]
        ms = _dot((zc * zc).astype(bf16), gmean_ref[...])
        outs.append(zc * lax.rsqrt(ms + EPS))
    return jnp.concatenate(outs, axis=-1) * gain


def _mixin_kernel(x_ref, mod_ref, g_ref, win_ref, gq_ref, gk_ref, gmean_ref, dc_ref, ds_ref,
                  q_ref, kt_ref, v_ref, xc_ref, xs_ref):
    x = x_ref[0]
    shift, scale = mod_ref[0, 0, 3:4, :], mod_ref[0, 0, 4:5, :]
    hb = _mod_norm(x, g_ref[...], shift, scale).astype(bf16)
    zq = _dot(hb, win_ref[:, 0:QK_W])
    q = (_group_rms(zq, gmean_ref, gq_ref[...]) * (ATTN_SCALE * LOG2E)).astype(bf16)
    first = lax.broadcasted_iota(jnp.int32, q.shape, 1) % HEAD_W < HEAD_DIM
    zero = jnp.zeros_like(q)
    q_ref[0, 0] = jnp.where(first, q, zero)
    q_ref[1, 0] = jnp.where(first, zero, q)
    zk = _dot(hb, win_ref[:, QK_W:2 * QK_W])
    kt_ref[0] = _group_rms(zk, gmean_ref, gk_ref[...]).T.astype(bf16)
    v = _dot(hb, win_ref[:, 2 * QK_W:2 * QK_W + V_W]).astype(bf16)
    ones = jnp.ones((v.shape[0], HEAD_W), bf16)
    for h in range(N_HEADS):
        v_ref[0, :, 2 * h * HEAD_W:(2 * h + 1) * HEAD_W] = v[:, h * HEAD_W:(h + 1) * HEAD_W]
        v_ref[0, :, (2 * h + 1) * HEAD_W:(2 * h + 2) * HEAD_W] = ones
    fb = _dot(hb, win_ref[:, 2 * QK_W + V_W:]).astype(bf16)
    for c in range(F_W // 256):
        fc = fb[:, c * 256:(c + 1) * 256]
        xc_ref[0, :, c * 256:(c + 1) * 256] = _dot(fc, dc_ref[...]).astype(bf16)
        xs_ref[0, :, c * 256:(c + 1) * 256] = _dot(fc, ds_ref[...]).astype(bf16)


def _mixin_call(x, mods, mrow, l, gain, w_in, gq, gk, gmean, dc, ds):
    b, s, d = x.shape
    tm = min(MIX_TM, s)
    row = lambda i, j: (i, j, 0)
    tok = lambda w: jax.ShapeDtypeStruct((b, s, w), bf16)
    return pl.pallas_call(
        _mixin_kernel,
        grid=(b, s // tm),
        in_specs=[
            pl.BlockSpec((1, tm, d), row),
            _mod_spec(l, mrow),
            _layer((1, d), l),
            _layer(w_in.shape[1:], l),
            _layer((1, QK_W), l),
            _layer((1, QK_W), l),
            _resident((256, 256)),
            _resident((256, 256)),
            _resident((256, 256)),
        ],
        out_specs=[
            pl.BlockSpec((2, 1, tm, QK_W), lambda i, j: (0, i, j, 0)),
            pl.BlockSpec((1, QK_W, tm), lambda i, j: (i, 0, j)),
            pl.BlockSpec((1, tm, 2 * V_W), row),
            pl.BlockSpec((1, tm, F_W), row),
            pl.BlockSpec((1, tm, F_W), row),
        ],
        out_shape=[jax.ShapeDtypeStruct((2, b, s, QK_W), bf16), jax.ShapeDtypeStruct((b, QK_W, s), bf16),
                   tok(2 * V_W), tok(F_W), tok(F_W)],
        compiler_params=_params(("parallel", "parallel")),
        name="mix_in",
    )(x, mods, gain, w_in, gq, gk, gmean, dc, ds)


def _attn_kernel(lqk_ref, q_ref, kt_ref, v_ref, bias_ref, o_ref, s_scr, m_scr, *, n_tiles, lambda_init):
    tq, tk = ATT_TQ, ATT_TK
    qt = pl.program_id(2)

    def score_map(j, carry):
        qj = q_ref[j, 0]
        m_run = jnp.full((tq, 128), -jnp.inf, bf16)
        for kt in range(n_tiles):
            tile = jnp.clip(kt - qt * (tq // tk) + 2, 0, N_BIAS_TILES - 1)
            sj = _dot(qj, kt_ref[0, :, kt * tk:(kt + 1) * tk]) + bias_ref[j, tile]
            s_scr[j, :, kt * tk:(kt + 1) * tk] = sj
            m_run = jnp.maximum(m_run, jnp.maximum(sj[:, :128], sj[:, 128:]).astype(bf16))
        m_scr[j] = m_run.astype(f32)
        return carry

    lax.fori_loop(0, 2, score_map, 0)

    heads = []
    for j in range(2):
        m = jnp.max(m_scr[j], axis=-1, keepdims=True)
        p = jnp.exp2((s_scr[j] - m).astype(bf16))
        ov = _dot(p, v_ref[0])
        heads.append(ov[:, :HEAD_W] / ov[:, HEAD_W:])

    lqk = lqk_ref[...]
    lam = (jnp.exp(jnp.sum(lqk[0:1] * lqk[1:2], axis=-1, keepdims=True))
           - jnp.exp(jnp.sum(lqk[2:3] * lqk[3:4], axis=-1, keepdims=True)) + lambda_init)
    o_ref[0] = (heads[0] - lam * heads[1]).astype(bf16)


def _attn_call(l, lqk, q, kt, vx, bias, lambda_init):
    _, b, s, _ = q.shape
    tq, tk = ATT_TQ, ATT_TK
    assert s % tk == 0 and s % tq == 0
    return pl.pallas_call(
        functools.partial(_attn_kernel, n_tiles=s // tk, lambda_init=lambda_init),
        grid=(b, N_HEADS, s // tq),
        in_specs=[
            _layer((4, HEAD_DIM), l),
            pl.BlockSpec((2, 1, tq, HEAD_W), lambda i, h, j: (0, i, j, h)),
            pl.BlockSpec((1, HEAD_W, s), lambda i, h, j: (i, h, 0)),
            pl.BlockSpec((1, s, 2 * HEAD_W), lambda i, h, j: (i, 0, h)),
            pl.BlockSpec((2, N_BIAS_TILES, tq, tk), lambda i, h, j: (h, 0, 0, 0)),
        ],
        out_specs=pl.BlockSpec((1, tq, HEAD_W), lambda i, h, j: (i, j, h)),
        out_shape=jax.ShapeDtypeStruct((b, s, V_W), bf16),
        scratch_shapes=[pltpu.VMEM((2, tq, s), f32), pltpu.VMEM((2, tq, 128), f32)],
        compiler_params=_params(("parallel", "parallel", "arbitrary")),
        name="diff_attn",
    )(lqk, q, kt, vx, bias)


def _mixout_kernel(x_ref, mod_ref, o_ref, g_ref, xc_ref, xs_ref, cmo_ref, smo_ref, cme_ref, sme_ref, wout_ref, y_ref,
                   odd_scr, even_scr, fo_scr, *, scale, quarter, lambda_init):
    @pl.when(pl.program_id(1) == 0)
    def _():
        for t, src in enumerate((xc_ref, xs_ref)):
            x0, x1, x2, x3 = (src[0, i * quarter:(i + 1) * quarter] for i in range(4))
            odd_scr[t, :quarter] = x0 - x2
            odd_scr[t, quarter:] = x1 - x3
            lo, hi = x0 + x2, x1 + x3
            even_scr[t, 0] = lo + hi
            even_scr[t, 1] = lo - hi

    tm = cmo_ref.shape[1]
    parts = [(c, _dot(cmo_ref[i], odd_scr[0]) - _dot(smo_ref[i], odd_scr[1])) for i, c in enumerate(DFT_CLASSES[:2])]
    parts += [(c, _dot(cme_ref[i], even_scr[0, i]) - _dot(sme_ref[i], even_scr[1, i]))
              for i, c in enumerate(DFT_CLASSES[2:])]
    for c, fo in parts:
        fo = fo * scale
        for ch in range(F_W // 128):
            fo_scr[ch, pl.ds(c, tm, stride=4), :] = fo[:, ch * 128:(ch + 1) * 128]
    fo = jnp.concatenate([fo_scr[ch] for ch in range(F_W // 128)], axis=-1).astype(bf16)
    heads = []
    for h in range(N_HEADS):
        oh = o_ref[0, :, h * HEAD_W:(h + 1) * HEAD_W].astype(f32)
        ms = jnp.mean(oh * oh, axis=-1, keepdims=True)
        heads.append(((oh * lax.rsqrt(ms + EPS) * g_ref[...]) * (1.0 - lambda_init)).astype(bf16))
    mix = _dot(jnp.concatenate(heads, axis=-1), wout_ref[0:V_W, :]) + _dot(fo, wout_ref[V_W:, :])
    y_ref[0] = x_ref[0] + mod_ref[0, 0, 5:6, :] * mix


def _mixout_call(x, mods, mrow, l, o, subln, xc, xs, cm, sm, w_out, lambda_init):
    b, s, d = x.shape
    quarter = s // 4
    tm = min(OUT_TM, quarter)
    row = lambda i, j: (i, j, 0)
    seq = pl.BlockSpec((1, s, F_W), lambda i, j: (i, 0, 0), pipeline_mode=pl.Buffered(1))
    odd = pl.BlockSpec((2, tm, 2 * quarter), lambda i, j: (0, j, 0))
    even = pl.BlockSpec((2, tm, quarter), lambda i, j: (1, j, 0))
    return pl.pallas_call(
        functools.partial(_mixout_kernel, scale=1.0 / math.sqrt(F_GROUP * s), quarter=quarter,
                          lambda_init=lambda_init),
        grid=(b, quarter // tm),
        in_specs=[
            pl.BlockSpec((1, 4 * tm, d), row),
            _mod_spec(l, mrow),
            pl.BlockSpec((1, 4 * tm, V_W), row),
            _layer((1, HEAD_W), l),
            seq, seq, odd, odd, even, even,
            _layer(w_out.shape[1:], l),
        ],
        out_specs=pl.BlockSpec((1, 4 * tm, d), row),
        out_shape=jax.ShapeDtypeStruct(x.shape, f32),
        scratch_shapes=[pltpu.VMEM((2, 2 * quarter, F_W), bf16), pltpu.VMEM((2, 2, quarter, F_W), bf16),
                        pltpu.VMEM((F_W // 128, 4 * tm, 128), f32)],
        compiler_params=_params(("parallel", "arbitrary")),
        name="mix_out",
    )(x, mods, o, subln, xc, xs, cm, sm, cm, sm, w_out)


def _dft_kernel(ac_ref, as_ref, bc_ref, bs_ref, cm_ref, sm_ref):
    bc, bsn = bc_ref[0], bs_ref[0]
    for i in range(ac_ref.shape[0]):
        ac, asn = ac_ref[i], as_ref[i]
        cm_ref[0, i * DFT_B:(i + 1) * DFT_B, :] = (ac * bc - asn * bsn).astype(bf16)
        sm_ref[0, i * DFT_B:(i + 1) * DFT_B, :] = (asn * bc + ac * bsn).astype(bf16)


def _dft_tables(s):
    half, quarter = s // 2, s // 4
    pos = np.arange(half, dtype=np.int64)
    ang = lambda freq: 2.0 * np.pi * ((freq[..., None] * pos) % s).astype(np.float64) / s
    a = ang(np.arange(quarter // DFT_B, dtype=np.int64) * (4 * DFT_B))[:, None, :]
    b = ang(4 * np.arange(DFT_B, dtype=np.int64)[None, :] + np.asarray(DFT_CLASSES, dtype=np.int64)[:, None])
    tab = lambda m: jnp.asarray(m.astype(np.float32))
    return tab(np.cos(a)), tab(np.sin(a)), tab(np.cos(b)), tab(np.sin(b))


def _dft_call(s):
    half, quarter = s // 2, s // 4
    a_per_step = min(DFT_A_PER_STEP, quarter // DFT_B)
    rows = DFT_B * a_per_step
    assert quarter % rows == 0
    ac, asn, bc, bsn = _dft_tables(s)
    aspec = pl.BlockSpec((a_per_step, 1, half), lambda c, a: (a, 0, 0))
    bspec = pl.BlockSpec((1, DFT_B, half), lambda c, a: (c, 0, 0))
    out = pl.BlockSpec((1, rows, half), lambda c, a: (c, a, 0))
    return pl.pallas_call(
        _dft_kernel,
        grid=(len(DFT_CLASSES), quarter // rows),
        in_specs=[aspec, aspec, bspec, bspec],
        out_specs=[out, out],
        out_shape=[jax.ShapeDtypeStruct((len(DFT_CLASSES), quarter, half), bf16)] * 2,
        compiler_params=_params(("parallel", "parallel")),
        name="dft_matrix",
    )(ac, asn, bc, bsn)


def _rel_bucket_np(rel):
    nb = NUM_BUCKETS // 2
    max_exact = nb // 2
    ret = (rel > 0).astype(np.int32) * nb
    n = np.abs(rel)
    nf = np.maximum(n, 1).astype(np.float32)
    ratio = np.log(nf / np.float32(max_exact)) / np.float32(math.log(REL_MAX_DISTANCE / max_exact))
    large = max_exact + (ratio * np.float32(nb - max_exact)).astype(np.int32)
    large = np.minimum(large, nb - 1)
    return (ret + np.where(n < max_exact, n, large)).astype(np.int32)


def _bucket_tiles():
    qpos = np.arange(ATT_TQ)[:, None]
    kpos = np.arange(ATT_TK)[None, :]
    return np.stack([_rel_bucket_np((d * ATT_TK + kpos) - qpos) for d in range(-2, N_BIAS_TILES - 2)])


def _bias_kernel(table_ref, bucket_ref, o_ref):
    n_maps = o_ref.shape[0]

    def rows(r, carry):
        r0 = pl.multiple_of(r * BIAS_ROWS, BIAS_ROWS)
        for c in range(ATT_TK // 128):
            bucket = bucket_ref[0, pl.ds(r0, BIAS_ROWS), c * 128:(c + 1) * 128]
            for hj in range(n_maps):
                row = jnp.broadcast_to(table_ref[hj:hj + 1, :], bucket.shape)
                o_ref[hj, 0, pl.ds(r0, BIAS_ROWS), c * 128:(c + 1) * 128] = (
                    jnp.take_along_axis(row, bucket, axis=1) * LOG2E)
        return carry

    lax.fori_loop(0, ATT_TQ // BIAS_ROWS, rows, 0)


def _bias_call(rel_bias):
    n_maps = 2 * N_HEADS
    table = jnp.pad(rel_bias.reshape(NUM_BUCKETS, n_maps).T, ((0, 0), (0, 128 - NUM_BUCKETS)))
    return pl.pallas_call(
        _bias_kernel,
        grid=(N_BIAS_TILES,),
        in_specs=[
            _resident((n_maps, 128)),
            pl.BlockSpec((1, ATT_TQ, ATT_TK), lambda t: (t, 0, 0)),
        ],
        out_specs=pl.BlockSpec((n_maps, 1, ATT_TQ, ATT_TK), lambda t: (0, t, 0, 0)),
        out_shape=jax.ShapeDtypeStruct((n_maps, N_BIAS_TILES, ATT_TQ, ATT_TK), f32),
        compiler_params=_params(("arbitrary",)),
        name="rel_bias_tiles",
    )(table, jnp.asarray(_bucket_tiles()))


def _channel_dft_tables():
    idx = np.arange(F_GROUP)
    ang = 2.0 * np.pi * ((idx[:, None] * idx[None, :]) % F_GROUP) / F_GROUP
    eye = np.eye(256 // F_GROUP)
    return (jnp.asarray(np.kron(eye, np.cos(ang)), dtype=bf16), jnp.asarray(np.kron(eye, np.sin(ang)), dtype=bf16))


def _group_mean_table():
    return jnp.asarray(np.kron(np.eye(256 // HEAD_DIM), np.full((HEAD_DIM, HEAD_DIM), 1.0 / HEAD_DIM)), dtype=bf16)


def _trunk(x, mods, mrow, p, bias, cm, sm, consts):
    gmean, dc, ds = consts
    depth = p["w_in"].shape[0]
    for l in range(depth):
        lambda_init = 0.8 - 0.6 * math.exp(-0.3 * l)
        x = _ffn_call(x, mods, mrow, l, p["norm_ffn1"], p["ffn1_wi"], p["ffn1_wo"], 0)
        q, kt, vx, xc, xs = _mixin_call(x, mods, mrow, l, p["norm_mix"], p["w_in"], p["q_norm"], p["k_norm"],
                                        gmean, dc, ds)
        o = _attn_call(l, p["lambda_qk"], q, kt, vx, bias, lambda_init)
        x = _mixout_call(x, mods, mrow, l, o, p["subln"], xc, xs, cm, sm, p["w_out"], lambda_init)
        x = _ffn_call(x, mods, mrow, l, p["norm_ffn2"], p["ffn2_wi"], p["ffn2_wo"], 6)
    return x


def kernel(x_prompt, x_sample, c_prompt, c_sample, ada_w, ada_b, norm_ffn1, norm_mix, norm_ffn2, ffn1_wi, ffn1_wo,
           ffn2_wi, ffn2_wo, w_in, w_out, q_norm, k_norm, lambda_qk, subln, rel_bias):
    depth = ada_w.shape[0]
    nb_p, nb_s = c_prompt.shape[0], c_sample.shape[0]
    assert nb_p + nb_s <= MOD_ROWS
    c_all = jnp.concatenate([c_prompt, c_sample, jnp.zeros((MOD_ROWS - nb_p - nb_s, D_MODEL), f32)], axis=0)
    mods = _mod_call(c_all, ada_w, ada_b).reshape(depth, MOD_ROWS, N_MOD, D_MODEL)

    row = lambda a: a.reshape(depth, 1, -1)
    tile_heads = lambda a: jnp.tile(a, (1, QK_W // HEAD_DIM)).reshape(depth, 1, QK_W)
    p = dict(
        norm_ffn1=row(norm_ffn1), norm_mix=row(norm_mix), norm_ffn2=row(norm_ffn2),
        ffn1_wi=ffn1_wi.astype(bf16), ffn1_wo=ffn1_wo.astype(bf16),
        ffn2_wi=ffn2_wi.astype(bf16), ffn2_wo=ffn2_wo.astype(bf16),
        w_in=w_in.astype(bf16), w_out=w_out.astype(bf16),
        q_norm=tile_heads(q_norm), k_norm=tile_heads(k_norm),
        lambda_qk=lambda_qk, subln=row(subln),
    )
    bias = _bias_call(rel_bias)
    consts = (_group_mean_table(),) + _channel_dft_tables()

    outs = []
    for x, mrow in ((x_prompt, 0), (x_sample, nb_p)):
        cm, sm = _dft_call(x.shape[1])
        outs.append(_trunk(x, mods, mrow, p, bias, cm, sm, consts))
    return tuple(outs)
```

```python
import functools
import math

import numpy as np
import jax
import jax.numpy as jnp
from jax import lax
from jax.experimental import pallas as pl
from jax.experimental.pallas import tpu as pltpu

D_MODEL = 1024
HEAD_DIM = 64
N_HEADS = 4
HEAD_W = 2 * HEAD_DIM
QK_W = N_HEADS * HEAD_W
V_W = QK_W
F_W = 512
F_GROUP = 64
D_FF = 2816
N_MOD = 9
NUM_BUCKETS = 32
REL_MAX_DISTANCE = 128
EPS = 1e-6
ATTN_SCALE = HEAD_DIM ** -0.5

V7X_VMEM_LIMIT_BYTES = 58 * 1024 * 1024
MOD_ROWS = 16
MOD_TN = 1024
FFN_TM = 1024
FFN_SUB = 512
FFN_CK = 256
MIX_TM = 512
ATT_TQ = 512
ATT_TK = 256
ATT_LOGIT_SCRATCH_BYTES = 16 * 1024 * 1024
N_BIAS_TILES = ATT_TQ // ATT_TK + 4
BIAS_ROWS = 128
LOG2E = math.log2(math.e)
OUT_TM = 256
MIXOUT_SEQ_INPUT_BYTES = 8 * 1024 * 1024
DFT_B = 64
DFT_A_PER_STEP = 4
DFT_CLASSES = (1, 3, 0, 2)

f32 = jnp.float32
bf16 = jnp.bfloat16


def _dot(a, b):
    return jnp.dot(a, b, preferred_element_type=f32)


def _params(sem, vmem=V7X_VMEM_LIMIT_BYTES):
    return pltpu.CompilerParams(dimension_semantics=sem, vmem_limit_bytes=vmem)


def _resident(shape):
    return pl.BlockSpec(shape, lambda *_: (0,) * len(shape), pipeline_mode=pl.Buffered(1))


def _layer(shape, l):
    return pl.BlockSpec((None,) + tuple(shape), lambda *_: (l,) + (0,) * len(shape), pipeline_mode=pl.Buffered(1))


def _split_bf16(a):
    hi = a.astype(bf16)
    lo = (a - hi.astype(f32)).astype(bf16)
    return hi, lo


def _mod_norm(x, gain, shift, scale):
    ms = jnp.mean(x * x, axis=-1, keepdims=True)
    return (x * lax.rsqrt(ms + EPS) * gain) * (1.0 + scale) + shift


def _mod_kernel(c_ref, w_ref, b_ref, o_ref):
    c = c_ref[...]
    sc = c * (1.0 / (1.0 + jnp.exp(-c)))
    s_hi, s_lo = _split_bf16(sc)
    w_hi, w_lo = _split_bf16(w_ref[0])
    o_ref[0] = _dot(s_hi, w_hi) + _dot(s_hi, w_lo) + _dot(s_lo, w_hi) + b_ref[0]


def _mod_call(c_all, ada_w, ada_b):
    depth, _, n = ada_w.shape
    return pl.pallas_call(
        _mod_kernel,
        grid=(depth, n // MOD_TN),
        in_specs=[
            pl.BlockSpec((MOD_ROWS, D_MODEL), lambda l, j: (0, 0)),
            pl.BlockSpec((1, D_MODEL, MOD_TN), lambda l, j: (l, 0, j)),
            pl.BlockSpec((1, 1, MOD_TN), lambda l, j: (l, 0, j)),
        ],
        out_specs=pl.BlockSpec((1, MOD_ROWS, MOD_TN), lambda l, j: (l, 0, j)),
        out_shape=jax.ShapeDtypeStruct((depth, MOD_ROWS, n), f32),
        compiler_params=_params(("arbitrary", "arbitrary")),
        name="adaln_mod",
    )(c_all, ada_w, ada_b.reshape(depth, 1, n))


def _ffn_kernel(x_ref, mod_ref, g_ref, wi_ref, wo_ref, o_ref, acc_ref, *, row0):
    shift, scale, gate = (mod_ref[0, 0, row0 + i:row0 + i + 1, :] for i in range(3))
    subs = [slice(r, r + FFN_SUB) for r in range(0, x_ref.shape[1], FFN_SUB)]
    hbs = [_mod_norm(x_ref[0, rows, :], g_ref[...], shift, scale).astype(bf16) for rows in subs]
    for c in range(D_FF // FFN_CK):
        for rows, hb in zip(subs, hbs):
            g = _dot(hb, wi_ref[:, c * FFN_CK:(c + 1) * FFN_CK])
            u = _dot(hb, wi_ref[:, D_FF + c * FFN_CK:D_FF + (c + 1) * FFN_CK])
            a = (g * (1.0 / (1.0 + jnp.exp(-g))) * u).astype(bf16)
            part = _dot(a, wo_ref[c * FFN_CK:(c + 1) * FFN_CK, :])
            if c == 0:
                acc_ref[rows, :] = part
            else:
                acc_ref[rows, :] += part
    for rows in subs:
        o_ref[0, rows, :] = x_ref[0, rows, :] + (0.5 * gate) * acc_ref[rows, :]


def _mod_spec(l, row0):
    return pl.BlockSpec((1, 1, N_MOD, D_MODEL), lambda i, *_: (l, row0 + i, 0, 0))


def _ffn_call(x, mods, mrow, l, gain, wi, wo, row0):
    b, s, d = x.shape
    tm = min(FFN_TM, s)
    return pl.pallas_call(
        functools.partial(_ffn_kernel, row0=row0),
        grid=(b, s // tm),
        in_specs=[
            pl.BlockSpec((1, tm, d), lambda i, j: (i, j, 0)),
            _mod_spec(l, mrow),
            _layer((1, d), l),
            _layer((d, 2 * D_FF), l),
            _layer((D_FF, d), l),
        ],
        out_specs=pl.BlockSpec((1, tm, d), lambda i, j: (i, j, 0)),
        out_shape=jax.ShapeDtypeStruct(x.shape, f32),
        scratch_shapes=[pltpu.VMEM((tm, d), f32)],
        compiler_params=_params(("parallel", "parallel")),
        name="ffn",
    )(x, mods, gain, wi, wo)


def _group_rms(z, gmean_ref, gain):
    outs = []
    for c in range(QK_W // 256):
        zc = z[:, c * 256:(c + 1) * 256]
        ms = _dot((zc * zc).astype(bf16), gmean_ref[...])
        outs.append(zc * lax.rsqrt(ms + EPS))
    return jnp.concatenate(outs, axis=-1) * gain


def _mixin_kernel(x_ref, mod_ref, g_ref, win_ref, gq_ref, gk_ref, gmean_ref, dc_ref, ds_ref,
                  q_ref, kt_ref, v_ref, xc_ref, xs_ref):
    x = x_ref[0]
    shift, scale = mod_ref[0, 0, 3:4, :], mod_ref[0, 0, 4:5, :]
    hb = _mod_norm(x, g_ref[...], shift, scale).astype(bf16)
    zq = _dot(hb, win_ref[:, 0:QK_W])
    q = (_group_rms(zq, gmean_ref, gq_ref[...]) * (ATTN_SCALE * LOG2E)).astype(bf16)
    first = lax.broadcasted_iota(jnp.int32, q.shape, 1) % HEAD_W < HEAD_DIM
    zero = jnp.zeros_like(q)
    q_ref[0, 0] = jnp.where(first, q, zero)
    q_ref[1, 0] = jnp.where(first, zero, q)
    zk = _dot(hb, win_ref[:, QK_W:2 * QK_W])
    kt_ref[0] = _group_rms(zk, gmean_ref, gk_ref[...]).T.astype(bf16)
    v = _dot(hb, win_ref[:, 2 * QK_W:2 * QK_W + V_W]).astype(bf16)
    ones = jnp.ones((v.shape[0], HEAD_W), bf16)
    for h in range(N_HEADS):
        v_ref[0, :, 2 * h * HEAD_W:(2 * h + 1) * HEAD_W] = v[:, h * HEAD_W:(h + 1) * HEAD_W]
        v_ref[0, :, (2 * h + 1) * HEAD_W:(2 * h + 2) * HEAD_W] = ones
    fb = _dot(hb, win_ref[:, 2 * QK_W + V_W:]).astype(bf16)
    for c in range(F_W // 256):
        fc = fb[:, c * 256:(c + 1) * 256]
        xc_ref[0, :, c * 256:(c + 1) * 256] = _dot(fc, dc_ref[...]).astype(bf16)
        xs_ref[0, :, c * 256:(c + 1) * 256] = _dot(fc, ds_ref[...]).astype(bf16)


def _mixin_call(x, mods, mrow, l, gain, w_in, gq, gk, gmean, dc, ds):
    b, s, d = x.shape
    tm = min(MIX_TM, s)
    row = lambda i, j: (i, j, 0)
    tok = lambda w: jax.ShapeDtypeStruct((b, s, w), bf16)
    return pl.pallas_call(
        _mixin_kernel,
        grid=(b, s // tm),
        in_specs=[
            pl.BlockSpec((1, tm, d), row),
            _mod_spec(l, mrow),
            _layer((1, d), l),
            _layer(w_in.shape[1:], l),
            _layer((1, QK_W), l),
            _layer((1, QK_W), l),
            _resident((256, 256)),
            _resident((256, 256)),
            _resident((256, 256)),
        ],
        out_specs=[
            pl.BlockSpec((2, 1, tm, QK_W), lambda i, j: (0, i, j, 0)),
            pl.BlockSpec((1, QK_W, tm), lambda i, j: (i, 0, j)),
            pl.BlockSpec((1, tm, 2 * V_W), row),
            pl.BlockSpec((1, tm, F_W), row),
            pl.BlockSpec((1, tm, F_W), row),
        ],
        out_shape=[jax.ShapeDtypeStruct((2, b, s, QK_W), bf16), jax.ShapeDtypeStruct((b, QK_W, s), bf16),
                   tok(2 * V_W), tok(F_W), tok(F_W)],
        compiler_params=_params(("parallel", "parallel")),
        name="mix_in",
    )(x, mods, gain, w_in, gq, gk, gmean, dc, ds)


def _attn_kernel(lqk_ref, q_ref, kt_ref, v_ref, bias_ref, o_ref, s_scr, m_scr, *, n_tiles, n_sub, lambda_init):
    tq, tk = ATT_TQ, ATT_TK
    qt = pl.program_id(2) * n_sub

    def score_map(i, carry):
        sub, j = i // 2, i % 2
        qj = q_ref[j, 0, pl.ds(pl.multiple_of(sub * tq, tq), tq), :]
        m_run = jnp.full((tq, 128), -jnp.inf, bf16)
        for kt in range(n_tiles):
            tile = jnp.clip(kt - (qt + sub) * (tq // tk) + 2, 0, N_BIAS_TILES - 1)
            sj = _dot(qj, kt_ref[0, :, kt * tk:(kt + 1) * tk]) + bias_ref[j, tile]
            s_scr[i, :, kt * tk:(kt + 1) * tk] = sj
            m_run = jnp.maximum(m_run, jnp.maximum(sj[:, :128], sj[:, 128:]).astype(bf16))
        m_scr[i] = m_run.astype(f32)
        return carry

    lax.fori_loop(0, 2 * n_sub, score_map, 0)

    lqk = lqk_ref[...]
    lam = (jnp.exp(jnp.sum(lqk[0:1] * lqk[1:2], axis=-1, keepdims=True))
           - jnp.exp(jnp.sum(lqk[2:3] * lqk[3:4], axis=-1, keepdims=True)) + lambda_init)
    for sub in range(n_sub):
        heads = []
        for i in (2 * sub, 2 * sub + 1):
            m = jnp.max(m_scr[i], axis=-1, keepdims=True)
            p = jnp.exp2((s_scr[i] - m).astype(bf16))
            ov = _dot(p, v_ref[0])
            heads.append(ov[:, :HEAD_W] / ov[:, HEAD_W:])
        o_ref[0, sub * tq:(sub + 1) * tq, :] = (heads[0] - lam * heads[1]).astype(bf16)


def _attn_call(l, lqk, q, kt, vx, bias, lambda_init):
    _, b, s, _ = q.shape
    tq, tk = ATT_TQ, ATT_TK
    assert s % tk == 0 and s % tq == 0
    n_sub = max(1, min(s // tq, ATT_LOGIT_SCRATCH_BYTES // (2 * tq * s * 4)))
    rows = n_sub * tq
    return pl.pallas_call(
        functools.partial(_attn_kernel, n_tiles=s // tk, n_sub=n_sub, lambda_init=lambda_init),
        grid=(b, N_HEADS, s // rows),
        in_specs=[
            _layer((4, HEAD_DIM), l),
            pl.BlockSpec((2, 1, rows, HEAD_W), lambda i, h, j: (0, i, j, h)),
            pl.BlockSpec((1, HEAD_W, s), lambda i, h, j: (i, h, 0)),
            pl.BlockSpec((1, s, 2 * HEAD_W), lambda i, h, j: (i, 0, h)),
            pl.BlockSpec((2, N_BIAS_TILES, tq, tk), lambda i, h, j: (h, 0, 0, 0)),
        ],
        out_specs=pl.BlockSpec((1, rows, HEAD_W), lambda i, h, j: (i, j, h)),
        out_shape=jax.ShapeDtypeStruct((b, s, V_W), bf16),
        scratch_shapes=[pltpu.VMEM((2 * n_sub, tq, s), f32), pltpu.VMEM((2 * n_sub, tq, 128), f32)],
        compiler_params=_params(("parallel", "parallel", "arbitrary")),
        name="diff_attn",
    )(lqk, q, kt, vx, bias)


def _mixout_kernel(x_ref, mod_ref, o_ref, g_ref, xc_ref, xs_ref, cmo_ref, smo_ref, cme_ref, sme_ref, wout_ref, y_ref,
                   odd_scr, even_scr, fo_scr, *, scale, quarter, lambda_init):
    @pl.when(pl.program_id(1) == 0)
    def _():
        for t, src in enumerate((xc_ref, xs_ref)):
            x0, x1, x2, x3 = (src[0, i * quarter:(i + 1) * quarter] for i in range(4))
            odd_scr[t, :quarter] = x0 - x2
            odd_scr[t, quarter:] = x1 - x3
            lo, hi = x0 + x2, x1 + x3
            even_scr[t, 0] = lo + hi
            even_scr[t, 1] = lo - hi

    tm = cmo_ref.shape[1]
    parts = [(c, _dot(cmo_ref[i], odd_scr[0]) - _dot(smo_ref[i], odd_scr[1])) for i, c in enumerate(DFT_CLASSES[:2])]
    parts += [(c, _dot(cme_ref[i], even_scr[0, i]) - _dot(sme_ref[i], even_scr[1, i]))
              for i, c in enumerate(DFT_CLASSES[2:])]
    for c, fo in parts:
        fo = fo * scale
        for ch in range(F_W // 128):
            fo_scr[ch, pl.ds(c, tm, stride=4), :] = fo[:, ch * 128:(ch + 1) * 128]
    fo = jnp.concatenate([fo_scr[ch] for ch in range(F_W // 128)], axis=-1).astype(bf16)
    heads = []
    for h in range(N_HEADS):
        oh = o_ref[0, :, h * HEAD_W:(h + 1) * HEAD_W].astype(f32)
        ms = jnp.mean(oh * oh, axis=-1, keepdims=True)
        heads.append(((oh * lax.rsqrt(ms + EPS) * g_ref[...]) * (1.0 - lambda_init)).astype(bf16))
    mix = _dot(jnp.concatenate(heads, axis=-1), wout_ref[0:V_W, :]) + _dot(fo, wout_ref[V_W:, :])
    y_ref[0] = x_ref[0] + mod_ref[0, 0, 5:6, :] * mix


def _mixout_call(x, mods, mrow, l, o, subln, xc, xs, cm, sm, w_out, lambda_init):
    b, s, d = x.shape
    quarter = s // 4
    tm = min(OUT_TM, quarter)
    row = lambda i, j: (i, j, 0)
    seq_buffers = 2 if 2 * 2 * s * F_W * 2 <= MIXOUT_SEQ_INPUT_BYTES else 1
    seq = pl.BlockSpec((1, s, F_W), lambda i, j: (i, 0, 0), pipeline_mode=pl.Buffered(seq_buffers))
    odd = pl.BlockSpec((2, tm, 2 * quarter), lambda i, j: (0, j, 0))
    even = pl.BlockSpec((2, tm, quarter), lambda i, j: (1, j, 0))
    return pl.pallas_call(
        functools.partial(_mixout_kernel, scale=1.0 / math.sqrt(F_GROUP * s), quarter=quarter,
                          lambda_init=lambda_init),
        grid=(b, quarter // tm),
        in_specs=[
            pl.BlockSpec((1, 4 * tm, d), row),
            _mod_spec(l, mrow),
            pl.BlockSpec((1, 4 * tm, V_W), row),
            _layer((1, HEAD_W), l),
            seq, seq, odd, odd, even, even,
            _layer(w_out.shape[1:], l),
        ],
        out_specs=pl.BlockSpec((1, 4 * tm, d), row),
        out_shape=jax.ShapeDtypeStruct(x.shape, f32),
        scratch_shapes=[pltpu.VMEM((2, 2 * quarter, F_W), bf16), pltpu.VMEM((2, 2, quarter, F_W), bf16),
                        pltpu.VMEM((F_W // 128, 4 * tm, 128), f32)],
        compiler_params=_params(("parallel", "arbitrary")),
        name="mix_out",
    )(x, mods, o, subln, xc, xs, cm, sm, cm, sm, w_out)


def _dft_kernel(ac_ref, as_ref, bc_ref, bs_ref, cm_ref, sm_ref):
    bc, bsn = bc_ref[0], bs_ref[0]
    for i in range(ac_ref.shape[0]):
        ac, asn = ac_ref[i], as_ref[i]
        cm_ref[0, i * DFT_B:(i + 1) * DFT_B, :] = (ac * bc - asn * bsn).astype(bf16)
        sm_ref[0, i * DFT_B:(i + 1) * DFT_B, :] = (asn * bc + ac * bsn).astype(bf16)


def _dft_tables(s):
    half, quarter = s // 2, s // 4
    pos = np.arange(half, dtype=np.int64)
    ang = lambda freq: 2.0 * np.pi * ((freq[..., None] * pos) % s).astype(np.float64) / s
    a = ang(np.arange(quarter // DFT_B, dtype=np.int64) * (4 * DFT_B))[:, None, :]
    b = ang(4 * np.arange(DFT_B, dtype=np.int64)[None, :] + np.asarray(DFT_CLASSES, dtype=np.int64)[:, None])
    tab = lambda m: jnp.asarray(m.astype(np.float32))
    return tab(np.cos(a)), tab(np.sin(a)), tab(np.cos(b)), tab(np.sin(b))


def _dft_call(s):
    half, quarter = s // 2, s // 4
    a_per_step = min(DFT_A_PER_STEP, quarter // DFT_B)
    rows = DFT_B * a_per_step
    assert quarter % rows == 0
    ac, asn, bc, bsn = _dft_tables(s)
    aspec = pl.BlockSpec((a_per_step, 1, half), lambda c, a: (a, 0, 0))
    bspec = pl.BlockSpec((1, DFT_B, half), lambda c, a: (c, 0, 0))
    out = pl.BlockSpec((1, rows, half), lambda c, a: (c, a, 0))
    return pl.pallas_call(
        _dft_kernel,
        grid=(len(DFT_CLASSES), quarter // rows),
        in_specs=[aspec, aspec, bspec, bspec],
        out_specs=[out, out],
        out_shape=[jax.ShapeDtypeStruct((len(DFT_CLASSES), quarter, half), bf16)] * 2,
        compiler_params=_params(("parallel", "parallel")),
        name="dft_matrix",
    )(ac, asn, bc, bsn)


def _rel_bucket_np(rel):
    nb = NUM_BUCKETS // 2
    max_exact = nb // 2
    ret = (rel > 0).astype(np.int32) * nb
    n = np.abs(rel)
    nf = np.maximum(n, 1).astype(np.float32)
    ratio = np.log(nf / np.float32(max_exact)) / np.float32(math.log(REL_MAX_DISTANCE / max_exact))
    large = max_exact + (ratio * np.float32(nb - max_exact)).astype(np.int32)
    large = np.minimum(large, nb - 1)
    return (ret + np.where(n < max_exact, n, large)).astype(np.int32)


def _bucket_tiles():
    qpos = np.arange(ATT_TQ)[:, None]
    kpos = np.arange(ATT_TK)[None, :]
    return np.stack([_rel_bucket_np((d * ATT_TK + kpos) - qpos) for d in range(-2, N_BIAS_TILES - 2)])


def _bias_kernel(table_ref, bucket_ref, o_ref):
    n_maps = o_ref.shape[0]

    def rows(r, carry):
        r0 = pl.multiple_of(r * BIAS_ROWS, BIAS_ROWS)
        for c in range(ATT_TK // 128):
            bucket = bucket_ref[0, pl.ds(r0, BIAS_ROWS), c * 128:(c + 1) * 128]
            for hj in range(n_maps):
                row = jnp.broadcast_to(table_ref[hj:hj + 1, :], bucket.shape)
                o_ref[hj, 0, pl.ds(r0, BIAS_ROWS), c * 128:(c + 1) * 128] = (
                    jnp.take_along_axis(row, bucket, axis=1) * LOG2E)
        return carry

    lax.fori_loop(0, ATT_TQ // BIAS_ROWS, rows, 0)


def _bias_call(rel_bias):
    n_maps = 2 * N_HEADS
    table = jnp.pad(rel_bias.reshape(NUM_BUCKETS, n_maps).T, ((0, 0), (0, 128 - NUM_BUCKETS)))
    return pl.pallas_call(
        _bias_kernel,
        grid=(N_BIAS_TILES,),
        in_specs=[
            _resident((n_maps, 128)),
            pl.BlockSpec((1, ATT_TQ, ATT_TK), lambda t: (t, 0, 0)),
        ],
        out_specs=pl.BlockSpec((n_maps, 1, ATT_TQ, ATT_TK), lambda t: (0, t, 0, 0)),
        out_shape=jax.ShapeDtypeStruct((n_maps, N_BIAS_TILES, ATT_TQ, ATT_TK), f32),
        compiler_params=_params(("arbitrary",)),
        name="rel_bias_tiles",
    )(table, jnp.asarray(_bucket_tiles()))


def _channel_dft_tables():
    idx = np.arange(F_GROUP)
    ang = 2.0 * np.pi * ((idx[:, None] * idx[None, :]) % F_GROUP) / F_GROUP
    eye = np.eye(256 // F_GROUP)
    return (jnp.asarray(np.kron(eye, np.cos(ang)), dtype=bf16), jnp.asarray(np.kron(eye, np.sin(ang)), dtype=bf16))


def _group_mean_table():
    return jnp.asarray(np.kron(np.eye(256 // HEAD_DIM), np.full((HEAD_DIM, HEAD_DIM), 1.0 / HEAD_DIM)), dtype=bf16)


def _trunk(x, mods, mrow, p, bias, cm, sm, consts):
    gmean, dc, ds = consts
    depth = p["w_in"].shape[0]
    for l in range(depth):
        lambda_init = 0.8 - 0.6 * math.exp(-0.3 * l)
        x = _ffn_call(x, mods, mrow, l, p["norm_ffn1"], p["ffn1_wi"], p["ffn1_wo"], 0)
        q, kt, vx, xc, xs = _mixin_call(x, mods, mrow, l, p["norm_mix"], p["w_in"], p["q_norm"], p["k_norm"],
                                        gmean, dc, ds)
        o = _attn_call(l, p["lambda_qk"], q, kt, vx, bias, lambda_init)
        x = _mixout_call(x, mods, mrow, l, o, p["subln"], xc, xs, cm, sm, p["w_out"], lambda_init)
        x = _ffn_call(x, mods, mrow, l, p["norm_ffn2"], p["ffn2_wi"], p["ffn2_wo"], 6)
    return x


def kernel(x_prompt, x_sample, c_prompt, c_sample, ada_w, ada_b, norm_ffn1, norm_mix, norm_ffn2, ffn1_wi, ffn1_wo,
           ffn2_wi, ffn2_wo, w_in, w_out, q_norm, k_norm, lambda_qk, subln, rel_bias):
    depth = ada_w.shape[0]
    nb_p, nb_s = c_prompt.shape[0], c_sample.shape[0]
    assert nb_p + nb_s <= MOD_ROWS
    c_all = jnp.concatenate([c_prompt, c_sample, jnp.zeros((MOD_ROWS - nb_p - nb_s, D_MODEL), f32)], axis=0)
    mods = _mod_call(c_all, ada_w, ada_b).reshape(depth, MOD_ROWS, N_MOD, D_MODEL)

    row = lambda a: a.reshape(depth, 1, -1)
    tile_heads = lambda a: jnp.tile(a, (1, QK_W // HEAD_DIM)).reshape(depth, 1, QK_W)
    p = dict(
        norm_ffn1=row(norm_ffn1), norm_mix=row(norm_mix), norm_ffn2=row(norm_ffn2),
        ffn1_wi=ffn1_wi.astype(bf16), ffn1_wo=ffn1_wo.astype(bf16),
        ffn2_wi=ffn2_wi.astype(bf16), ffn2_wo=ffn2_wo.astype(bf16),
        w_in=w_in.astype(bf16), w_out=w_out.astype(bf16),
        q_norm=tile_heads(q_norm), k_norm=tile_heads(k_norm),
        lambda_qk=lambda_qk, subln=row(subln),
    )
    bias = _bias_call(rel_bias)
    consts = (_group_mean_table(),) + _channel_dft_tables()

    outs = []
    for x, mrow in ((x_prompt, 0), (x_sample, nb_p)):
        cm, sm = _dft_call(x.shape[1])
        outs.append(_trunk(x, mods, mrow, p, bias, cm, sm, consts))
    return tuple(outs)
```

```python
import functools
import math

import numpy as np
import jax
import jax.numpy as jnp
from jax import lax
from jax.experimental import pallas as pl
from jax.experimental.pallas import tpu as pltpu

D_MODEL = 1024
HEAD_DIM = 64
N_HEADS = 4
HEAD_W = 2 * HEAD_DIM
QK_W = N_HEADS * HEAD_W
V_W = QK_W
F_W = 512
F_GROUP = 64
D_FF = 2816
N_MOD = 9
NUM_BUCKETS = 32
REL_MAX_DISTANCE = 128
EPS = 1e-6
ATTN_SCALE = HEAD_DIM ** -0.5

V7X_VMEM_LIMIT_BYTES = 58 * 1024 * 1024
MOD_ROWS = 16
MOD_TN = 2304
FFN_TM = 1024
FFN_SUB = 512
FFN_CK = 256
MIX_TM = 1024
MIX_SUB = 512
ATT_TQ = 512
ATT_TK = 256
ATT_LOGIT_SCRATCH_BYTES = 16 * 1024 * 1024
N_BIAS_TILES = ATT_TQ // ATT_TK + 4
BIAS_ROWS = 128
LOG2E = math.log2(math.e)
OUT_TM = 256
MIXOUT_SEQ_INPUT_BYTES = 8 * 1024 * 1024
DFT_B = 64
DFT_A_PER_STEP = 4
DFT_CLASSES = (1, 3, 0, 2)

f32 = jnp.float32
bf16 = jnp.bfloat16


def _dot(a, b):
    return jnp.dot(a, b, preferred_element_type=f32)


def _params(sem, vmem=V7X_VMEM_LIMIT_BYTES):
    return pltpu.CompilerParams(dimension_semantics=sem, vmem_limit_bytes=vmem)


def _resident(shape):
    return pl.BlockSpec(shape, lambda *_: (0,) * len(shape), pipeline_mode=pl.Buffered(1))


def _layer(shape, l):
    return pl.BlockSpec((None,) + tuple(shape), lambda *_: (l,) + (0,) * len(shape), pipeline_mode=pl.Buffered(1))


def _split_bf16(a):
    hi = a.astype(bf16)
    lo = (a - hi.astype(f32)).astype(bf16)
    return hi, lo


def _mod_norm(x, gain, shift, scale):
    ms = jnp.mean(x * x, axis=-1, keepdims=True)
    return (x * lax.rsqrt(ms + EPS) * gain) * (1.0 + scale) + shift


def _mod_kernel(c_ref, w_ref, b_ref, o_ref):
    c = c_ref[...]
    sc = c * (1.0 / (1.0 + jnp.exp(-c)))
    s_hi, s_lo = _split_bf16(sc)
    w_hi, w_lo = _split_bf16(w_ref[0])
    o_ref[0] = _dot(s_hi, w_hi) + _dot(s_hi, w_lo) + _dot(s_lo, w_hi) + b_ref[0]


def _mod_call(c_all, ada_w, ada_b):
    depth, _, n = ada_w.shape
    return pl.pallas_call(
        _mod_kernel,
        grid=(depth, n // MOD_TN),
        in_specs=[
            pl.BlockSpec((MOD_ROWS, D_MODEL), lambda l, j: (0, 0)),
            pl.BlockSpec((1, D_MODEL, MOD_TN), lambda l, j: (l, 0, j)),
            pl.BlockSpec((1, 1, MOD_TN), lambda l, j: (l, 0, j)),
        ],
        out_specs=pl.BlockSpec((1, MOD_ROWS, MOD_TN), lambda l, j: (l, 0, j)),
        out_shape=jax.ShapeDtypeStruct((depth, MOD_ROWS, n), f32),
        compiler_params=_params(("arbitrary", "arbitrary")),
        name="adaln_mod",
    )(c_all, ada_w, ada_b.reshape(depth, 1, n))


def _ffn_kernel(x_ref, mod_ref, g_ref, wi_ref, wo_ref, o_ref, acc_ref, *, row0):
    shift, scale, gate = (mod_ref[0, 0, row0 + i:row0 + i + 1, :] for i in range(3))
    subs = [slice(r, r + FFN_SUB) for r in range(0, x_ref.shape[1], FFN_SUB)]
    hbs = [_mod_norm(x_ref[0, rows, :], g_ref[...], shift, scale).astype(bf16) for rows in subs]
    for c in range(D_FF // FFN_CK):
        for rows, hb in zip(subs, hbs):
            g = _dot(hb, wi_ref[:, c * FFN_CK:(c + 1) * FFN_CK])
            u = _dot(hb, wi_ref[:, D_FF + c * FFN_CK:D_FF + (c + 1) * FFN_CK])
            a = (g * (1.0 / (1.0 + jnp.exp(-g))) * u).astype(bf16)
            part = _dot(a, wo_ref[c * FFN_CK:(c + 1) * FFN_CK, :])
            if c == 0:
                acc_ref[rows, :] = part
            else:
                acc_ref[rows, :] += part
    for rows in subs:
        o_ref[0, rows, :] = x_ref[0, rows, :] + (0.5 * gate) * acc_ref[rows, :]


def _mod_spec(l, row0):
    return pl.BlockSpec((1, 1, N_MOD, D_MODEL), lambda i, *_: (l, row0 + i, 0, 0))


def _ffn_call(x, mods, mrow, l, gain, wi, wo, row0):
    b, s, d = x.shape
    tm = min(FFN_TM, s)
    return pl.pallas_call(
        functools.partial(_ffn_kernel, row0=row0),
        grid=(b, s // tm),
        in_specs=[
            pl.BlockSpec((1, tm, d), lambda i, j: (i, j, 0)),
            _mod_spec(l, mrow),
            _layer((1, d), l),
            _layer((d, 2 * D_FF), l),
            _layer((D_FF, d), l),
        ],
        out_specs=pl.BlockSpec((1, tm, d), lambda i, j: (i, j, 0)),
        out_shape=jax.ShapeDtypeStruct(x.shape, f32),
        scratch_shapes=[pltpu.VMEM((tm, d), f32)],
        compiler_params=_params(("parallel", "parallel")),
        name="ffn",
    )(x, mods, gain, wi, wo)


def _group_rms(z, gmean_ref, gain):
    outs = []
    for c in range(QK_W // 256):
        zc = z[:, c * 256:(c + 1) * 256]
        ms = _dot((zc * zc).astype(bf16), gmean_ref[...])
        outs.append(zc * lax.rsqrt(ms + EPS))
    return jnp.concatenate(outs, axis=-1) * gain


def _mixin_kernel(x_ref, mod_ref, g_ref, win_ref, gq_ref, gk_ref, gmean_ref, dc_ref, ds_ref,
                  q_ref, kt_ref, v_ref, xc_ref, xs_ref):
    shift, scale = mod_ref[0, 0, 3:4, :], mod_ref[0, 0, 4:5, :]
    for r0 in range(0, x_ref.shape[1], MIX_SUB):
        rows = slice(r0, r0 + MIX_SUB)
        hb = _mod_norm(x_ref[0, rows, :], g_ref[...], shift, scale).astype(bf16)
        zq = _dot(hb, win_ref[:, 0:QK_W])
        q = (_group_rms(zq, gmean_ref, gq_ref[...]) * (ATTN_SCALE * LOG2E)).astype(bf16)
        first = lax.broadcasted_iota(jnp.int32, q.shape, 1) % HEAD_W < HEAD_DIM
        zero = jnp.zeros_like(q)
        q_ref[0, 0, rows, :] = jnp.where(first, q, zero)
        q_ref[1, 0, rows, :] = jnp.where(first, zero, q)
        zk = _dot(hb, win_ref[:, QK_W:2 * QK_W])
        kt_ref[0, :, rows] = _group_rms(zk, gmean_ref, gk_ref[...]).T.astype(bf16)
        v = _dot(hb, win_ref[:, 2 * QK_W:2 * QK_W + V_W]).astype(bf16)
        ones = jnp.ones((MIX_SUB, HEAD_W), bf16)
        for h in range(N_HEADS):
            v_ref[0, rows, 2 * h * HEAD_W:(2 * h + 1) * HEAD_W] = v[:, h * HEAD_W:(h + 1) * HEAD_W]
            v_ref[0, rows, (2 * h + 1) * HEAD_W:(2 * h + 2) * HEAD_W] = ones
        fb = _dot(hb, win_ref[:, 2 * QK_W + V_W:]).astype(bf16)
        for c in range(F_W // 256):
            fc = fb[:, c * 256:(c + 1) * 256]
            xc_ref[0, rows, c * 256:(c + 1) * 256] = _dot(fc, dc_ref[...]).astype(bf16)
            xs_ref[0, rows, c * 256:(c + 1) * 256] = _dot(fc, ds_ref[...]).astype(bf16)


def _mixin_call(x, mods, mrow, l, gain, w_in, gq, gk, gmean, dc, ds):
    b, s, d = x.shape
    tm = min(MIX_TM, s)
    row = lambda i, j: (i, j, 0)
    tok = lambda w: jax.ShapeDtypeStruct((b, s, w), bf16)
    return pl.pallas_call(
        _mixin_kernel,
        grid=(b, s // tm),
        in_specs=[
            pl.BlockSpec((1, tm, d), row),
            _mod_spec(l, mrow),
            _layer((1, d), l),
            _layer(w_in.shape[1:], l),
            _layer((1, QK_W), l),
            _layer((1, QK_W), l),
            _resident((256, 256)),
            _resident((256, 256)),
            _resident((256, 256)),
        ],
        out_specs=[
            pl.BlockSpec((2, 1, tm, QK_W), lambda i, j: (0, i, j, 0)),
            pl.BlockSpec((1, QK_W, tm), lambda i, j: (i, 0, j)),
            pl.BlockSpec((1, tm, 2 * V_W), row),
            pl.BlockSpec((1, tm, F_W), row),
            pl.BlockSpec((1, tm, F_W), row),
        ],
        out_shape=[jax.ShapeDtypeStruct((2, b, s, QK_W), bf16), jax.ShapeDtypeStruct((b, QK_W, s), bf16),
                   tok(2 * V_W), tok(F_W), tok(F_W)],
        compiler_params=_params(("parallel", "parallel")),
        name="mix_in",
    )(x, mods, gain, w_in, gq, gk, gmean, dc, ds)


def _attn_kernel(lqk_ref, q_ref, kt_ref, v_ref, bias_ref, o_ref, s_scr, m_scr, *, n_tiles, n_sub, lambda_init):
    tq, tk = ATT_TQ, ATT_TK
    qt = pl.program_id(2) * n_sub

    def score_map(i, carry):
        sub, j = i // 2, i % 2
        qj = q_ref[j, 0, pl.ds(pl.multiple_of(sub * tq, tq), tq), :]
        m_run = jnp.full((tq, 128), -jnp.inf, bf16)
        for kt in range(n_tiles):
            tile = jnp.clip(kt - (qt + sub) * (tq // tk) + 2, 0, N_BIAS_TILES - 1)
            sj = _dot(qj, kt_ref[0, :, kt * tk:(kt + 1) * tk]) + bias_ref[j, tile]
            s_scr[i, :, kt * tk:(kt + 1) * tk] = sj
            m_run = jnp.maximum(m_run, jnp.maximum(sj[:, :128], sj[:, 128:]).astype(bf16))
        m_scr[i] = m_run.astype(f32)
        return carry

    lax.fori_loop(0, 2 * n_sub, score_map, 0)

    lqk = lqk_ref[...]
    lam = (jnp.exp(jnp.sum(lqk[0:1] * lqk[1:2], axis=-1, keepdims=True))
           - jnp.exp(jnp.sum(lqk[2:3] * lqk[3:4], axis=-1, keepdims=True)) + lambda_init)
    for sub in range(n_sub):
        heads = []
        for i in (2 * sub, 2 * sub + 1):
            m = jnp.max(m_scr[i], axis=-1, keepdims=True)
            p = jnp.exp2((s_scr[i] - m).astype(bf16))
            ov = _dot(p, v_ref[0])
            heads.append(ov[:, :HEAD_W] / ov[:, HEAD_W:])
        o_ref[0, sub * tq:(sub + 1) * tq, :] = (heads[0] - lam * heads[1]).astype(bf16)


def _attn_call(l, lqk, q, kt, vx, bias, lambda_init):
    _, b, s, _ = q.shape
    tq, tk = ATT_TQ, ATT_TK
    assert s % tk == 0 and s % tq == 0
    n_sub = max(1, min(s // tq, ATT_LOGIT_SCRATCH_BYTES // (2 * tq * s * 4)))
    rows = n_sub * tq
    return pl.pallas_call(
        functools.partial(_attn_kernel, n_tiles=s // tk, n_sub=n_sub, lambda_init=lambda_init),
        grid=(b, N_HEADS, s // rows),
        in_specs=[
            _layer((4, HEAD_DIM), l),
            pl.BlockSpec((2, 1, rows, HEAD_W), lambda i, h, j: (0, i, j, h)),
            pl.BlockSpec((1, HEAD_W, s), lambda i, h, j: (i, h, 0)),
            pl.BlockSpec((1, s, 2 * HEAD_W), lambda i, h, j: (i, 0, h)),
            pl.BlockSpec((2, N_BIAS_TILES, tq, tk), lambda i, h, j: (h, 0, 0, 0)),
        ],
        out_specs=pl.BlockSpec((1, rows, HEAD_W), lambda i, h, j: (i, j, h)),
        out_shape=jax.ShapeDtypeStruct((b, s, V_W), bf16),
        scratch_shapes=[pltpu.VMEM((2 * n_sub, tq, s), f32), pltpu.VMEM((2 * n_sub, tq, 128), f32)],
        compiler_params=_params(("parallel", "parallel", "arbitrary")),
        name="diff_attn",
    )(lqk, q, kt, vx, bias)


def _mixout_kernel(x_ref, mod_ref, o_ref, g_ref, xc_ref, xs_ref, cmo_ref, smo_ref, cme_ref, sme_ref, wout_ref, y_ref,
                   odd_scr, even_scr, fo_scr, *, scale, quarter, lambda_init):
    @pl.when(pl.program_id(1) == 0)
    def _():
        for t, src in enumerate((xc_ref, xs_ref)):
            x0, x1, x2, x3 = (src[0, i * quarter:(i + 1) * quarter] for i in range(4))
            odd_scr[t, :quarter] = x0 - x2
            odd_scr[t, quarter:] = x1 - x3
            lo, hi = x0 + x2, x1 + x3
            even_scr[t, 0] = lo + hi
            even_scr[t, 1] = lo - hi

    tm = cmo_ref.shape[1]
    parts = [(c, _dot(cmo_ref[i], odd_scr[0]) - _dot(smo_ref[i], odd_scr[1])) for i, c in enumerate(DFT_CLASSES[:2])]
    parts += [(c, _dot(cme_ref[i], even_scr[0, i]) - _dot(sme_ref[i], even_scr[1, i]))
              for i, c in enumerate(DFT_CLASSES[2:])]
    for c, fo in parts:
        fo = fo * scale
        for ch in range(F_W // 128):
            fo_scr[ch, pl.ds(c, tm, stride=4), :] = fo[:, ch * 128:(ch + 1) * 128]
    fo = jnp.concatenate([fo_scr[ch] for ch in range(F_W // 128)], axis=-1).astype(bf16)
    heads = []
    for h in range(N_HEADS):
        oh = o_ref[0, :, h * HEAD_W:(h + 1) * HEAD_W].astype(f32)
        ms = jnp.mean(oh * oh, axis=-1, keepdims=True)
        heads.append(((oh * lax.rsqrt(ms + EPS) * g_ref[...]) * (1.0 - lambda_init)).astype(bf16))
    mix = _dot(jnp.concatenate(heads, axis=-1), wout_ref[0:V_W, :]) + _dot(fo, wout_ref[V_W:, :])
    y_ref[0] = x_ref[0] + mod_ref[0, 0, 5:6, :] * mix


def _mixout_call(x, mods, mrow, l, o, subln, xc, xs, cm, sm, w_out, lambda_init):
    b, s, d = x.shape
    quarter = s // 4
    tm = min(OUT_TM, quarter)
    row = lambda i, j: (i, j, 0)
    seq_buffers = 2 if 2 * 2 * s * F_W * 2 <= MIXOUT_SEQ_INPUT_BYTES else 1
    seq = pl.BlockSpec((1, s, F_W), lambda i, j: (i, 0, 0), pipeline_mode=pl.Buffered(seq_buffers))
    odd = pl.BlockSpec((2, tm, 2 * quarter), lambda i, j: (0, j, 0))
    even = pl.BlockSpec((2, tm, quarter), lambda i, j: (1, j, 0))
    return pl.pallas_call(
        functools.partial(_mixout_kernel, scale=1.0 / math.sqrt(F_GROUP * s), quarter=quarter,
                          lambda_init=lambda_init),
        grid=(b, quarter // tm),
        in_specs=[
            pl.BlockSpec((1, 4 * tm, d), row),
            _mod_spec(l, mrow),
            pl.BlockSpec((1, 4 * tm, V_W), row),
            _layer((1, HEAD_W), l),
            seq, seq, odd, odd, even, even,
            _layer(w_out.shape[1:], l),
        ],
        out_specs=pl.BlockSpec((1, 4 * tm, d), row),
        out_shape=jax.ShapeDtypeStruct(x.shape, f32),
        scratch_shapes=[pltpu.VMEM((2, 2 * quarter, F_W), bf16), pltpu.VMEM((2, 2, quarter, F_W), bf16),
                        pltpu.VMEM((F_W // 128, 4 * tm, 128), f32)],
        compiler_params=_params(("parallel", "arbitrary")),
        name="mix_out",
    )(x, mods, o, subln, xc, xs, cm, sm, cm, sm, w_out)


def _dft_kernel(ac_ref, as_ref, bc_ref, bs_ref, cm_ref, sm_ref):
    bc, bsn = bc_ref[0], bs_ref[0]
    for i in range(ac_ref.shape[0]):
        ac, asn = ac_ref[i], as_ref[i]
        cm_ref[0, i * DFT_B:(i + 1) * DFT_B, :] = (ac * bc - asn * bsn).astype(bf16)
        sm_ref[0, i * DFT_B:(i + 1) * DFT_B, :] = (asn * bc + ac * bsn).astype(bf16)


def _dft_tables(s):
    half, quarter = s // 2, s // 4
    pos = np.arange(half, dtype=np.int64)
    ang = lambda freq: 2.0 * np.pi * ((freq[..., None] * pos) % s).astype(np.float64) / s
    a = ang(np.arange(quarter // DFT_B, dtype=np.int64) * (4 * DFT_B))[:, None, :]
    b = ang(4 * np.arange(DFT_B, dtype=np.int64)[None, :] + np.asarray(DFT_CLASSES, dtype=np.int64)[:, None])
    tab = lambda m: jnp.asarray(m.astype(np.float32))
    return tab(np.cos(a)), tab(np.sin(a)), tab(np.cos(b)), tab(np.sin(b))


def _dft_call(s):
    half, quarter = s // 2, s // 4
    a_per_step = min(DFT_A_PER_STEP, quarter // DFT_B)
    rows = DFT_B * a_per_step
    assert quarter % rows == 0
    ac, asn, bc, bsn = _dft_tables(s)
    aspec = pl.BlockSpec((a_per_step, 1, half), lambda c, a: (a, 0, 0))
    bspec = pl.BlockSpec((1, DFT_B, half), lambda c, a: (c, 0, 0))
    out = pl.BlockSpec((1, rows, half), lambda c, a: (c, a, 0))
    return pl.pallas_call(
        _dft_kernel,
        grid=(len(DFT_CLASSES), quarter // rows),
        in_specs=[aspec, aspec, bspec, bspec],
        out_specs=[out, out],
        out_shape=[jax.ShapeDtypeStruct((len(DFT_CLASSES), quarter, half), bf16)] * 2,
        compiler_params=_params(("parallel", "parallel")),
        name="dft_matrix",
    )(ac, asn, bc, bsn)


def _rel_bucket_np(rel):
    nb = NUM_BUCKETS // 2
    max_exact = nb // 2
    ret = (rel > 0).astype(np.int32) * nb
    n = np.abs(rel)
    nf = np.maximum(n, 1).astype(np.float32)
    ratio = np.log(nf / np.float32(max_exact)) / np.float32(math.log(REL_MAX_DISTANCE / max_exact))
    large = max_exact + (ratio * np.float32(nb - max_exact)).astype(np.int32)
    large = np.minimum(large, nb - 1)
    return (ret + np.where(n < max_exact, n, large)).astype(np.int32)


def _bucket_tiles():
    qpos = np.arange(ATT_TQ)[:, None]
    kpos = np.arange(ATT_TK)[None, :]
    return np.stack([_rel_bucket_np((d * ATT_TK + kpos) - qpos) for d in range(-2, N_BIAS_TILES - 2)])


def _bias_kernel(table_ref, bucket_ref, o_ref):
    n_maps = o_ref.shape[0]

    def rows(r, carry):
        r0 = pl.multiple_of(r * BIAS_ROWS, BIAS_ROWS)
        for c in range(ATT_TK // 128):
            bucket = bucket_ref[0, pl.ds(r0, BIAS_ROWS), c * 128:(c + 1) * 128]
            for hj in range(n_maps):
                row = jnp.broadcast_to(table_ref[hj:hj + 1, :], bucket.shape)
                o_ref[hj, 0, pl.ds(r0, BIAS_ROWS), c * 128:(c + 1) * 128] = (
                    jnp.take_along_axis(row, bucket, axis=1) * LOG2E)
        return carry

    lax.fori_loop(0, ATT_TQ // BIAS_ROWS, rows, 0)


def _bias_call(rel_bias):
    n_maps = 2 * N_HEADS
    table = jnp.pad(rel_bias.reshape(NUM_BUCKETS, n_maps).T, ((0, 0), (0, 128 - NUM_BUCKETS)))
    return pl.pallas_call(
        _bias_kernel,
        grid=(N_BIAS_TILES,),
        in_specs=[
            _resident((n_maps, 128)),
            pl.BlockSpec((1, ATT_TQ, ATT_TK), lambda t: (t, 0, 0)),
        ],
        out_specs=pl.BlockSpec((n_maps, 1, ATT_TQ, ATT_TK), lambda t: (0, t, 0, 0)),
        out_shape=jax.ShapeDtypeStruct((n_maps, N_BIAS_TILES, ATT_TQ, ATT_TK), f32),
        compiler_params=_params(("arbitrary",)),
        name="rel_bias_tiles",
    )(table, jnp.asarray(_bucket_tiles()))


def _channel_dft_tables():
    idx = np.arange(F_GROUP)
    ang = 2.0 * np.pi * ((idx[:, None] * idx[None, :]) % F_GROUP) / F_GROUP
    eye = np.eye(256 // F_GROUP)
    return (jnp.asarray(np.kron(eye, np.cos(ang)), dtype=bf16), jnp.asarray(np.kron(eye, np.sin(ang)), dtype=bf16))


def _group_mean_table():
    return jnp.asarray(np.kron(np.eye(256 // HEAD_DIM), np.full((HEAD_DIM, HEAD_DIM), 1.0 / HEAD_DIM)), dtype=bf16)


def _trunk(x, mods, mrow, p, bias, cm, sm, consts):
    gmean, dc, ds = consts
    depth = p["w_in"].shape[0]
    for l in range(depth):
        lambda_init = 0.8 - 0.6 * math.exp(-0.3 * l)
        x = _ffn_call(x, mods, mrow, l, p["norm_ffn1"], p["ffn1_wi"], p["ffn1_wo"], 0)
        q, kt, vx, xc, xs = _mixin_call(x, mods, mrow, l, p["norm_mix"], p["w_in"], p["q_norm"], p["k_norm"],
                                        gmean, dc, ds)
        o = _attn_call(l, p["lambda_qk"], q, kt, vx, bias, lambda_init)
        x = _mixout_call(x, mods, mrow, l, o, p["subln"], xc, xs, cm, sm, p["w_out"], lambda_init)
        x = _ffn_call(x, mods, mrow, l, p["norm_ffn2"], p["ffn2_wi"], p["ffn2_wo"], 6)
    return x


def kernel(x_prompt, x_sample, c_prompt, c_sample, ada_w, ada_b, norm_ffn1, norm_mix, norm_ffn2, ffn1_wi, ffn1_wo,
           ffn2_wi, ffn2_wo, w_in, w_out, q_norm, k_norm, lambda_qk, subln, rel_bias):
    depth = ada_w.shape[0]
    nb_p, nb_s = c_prompt.shape[0], c_sample.shape[0]
    assert nb_p + nb_s <= MOD_ROWS
    c_all = jnp.concatenate([c_prompt, c_sample, jnp.zeros((MOD_ROWS - nb_p - nb_s, D_MODEL), f32)], axis=0)
    mods = _mod_call(c_all, ada_w, ada_b).reshape(depth, MOD_ROWS, N_MOD, D_MODEL)

    row = lambda a: a.reshape(depth, 1, -1)
    tile_heads = lambda a: jnp.tile(a, (1, QK_W // HEAD_DIM)).reshape(depth, 1, QK_W)
    p = dict(
        norm_ffn1=row(norm_ffn1), norm_mix=row(norm_mix), norm_ffn2=row(norm_ffn2),
        ffn1_wi=ffn1_wi.astype(bf16), ffn1_wo=ffn1_wo.astype(bf16),
        ffn2_wi=ffn2_wi.astype(bf16), ffn2_wo=ffn2_wo.astype(bf16),
        w_in=w_in.astype(bf16), w_out=w_out.astype(bf16),
        q_norm=tile_heads(q_norm), k_norm=tile_heads(k_norm),
        lambda_qk=lambda_qk, subln=row(subln),
    )
    bias = _bias_call(rel_bias)
    consts = (_group_mean_table(),) + _channel_dft_tables()

    outs = []
    for x, mrow in ((x_prompt, 0), (x_sample, nb_p)):
        cm, sm = _dft_call(x.shape[1])
        outs.append(_trunk(x, mods, mrow, p, bias, cm, sm, consts))
    return tuple(outs)
```

```python
import functools
import math

import numpy as np
import jax
import jax.numpy as jnp
from jax import lax
from jax.experimental import pallas as pl
from jax.experimental.pallas import tpu as pltpu

D_MODEL = 1024
HEAD_DIM = 64
N_HEADS = 4
HEAD_W = 2 * HEAD_DIM
QK_W = N_HEADS * HEAD_W
V_W = QK_W
F_W = 512
F_GROUP = 64
D_FF = 2816
N_MOD = 9
NUM_BUCKETS = 32
REL_MAX_DISTANCE = 128
EPS = 1e-6
ATTN_SCALE = HEAD_DIM ** -0.5

V7X_VMEM_LIMIT_BYTES = 58 * 1024 * 1024
V7X_LANES = 128
V7X_MXU_DIM = 256
MOD_ROWS = 16
MOD_TN = 2304
FFN_TM = 1024
FFN_SUB = 512
FFN_CK = V7X_MXU_DIM
MIX_TM = 1024
MIX_SUB = 512
ATT_TQ = 512
ATT_TK = V7X_MXU_DIM
ATT_LOGIT_SCRATCH_BYTES = 16 * 1024 * 1024
N_BIAS_TILES = ATT_TQ // ATT_TK + 4
BIAS_ROWS = 128
LOG2E = math.log2(math.e)
OUT_TM = V7X_MXU_DIM
MIXOUT_SEQ_INPUT_BYTES = 8 * 1024 * 1024
DFT_B = 64
DFT_A_PER_STEP = 4
DFT_CLASSES = (1, 3, 0, 2)

f32 = jnp.float32
bf16 = jnp.bfloat16


def _dot(a, b):
    return jnp.dot(a, b, preferred_element_type=f32)


def _params(sem, vmem=V7X_VMEM_LIMIT_BYTES):
    return pltpu.CompilerParams(dimension_semantics=sem, vmem_limit_bytes=vmem)


def _resident(shape):
    return pl.BlockSpec(shape, lambda *_: (0,) * len(shape), pipeline_mode=pl.Buffered(1))


def _layer(shape, l):
    return pl.BlockSpec((None,) + tuple(shape), lambda *_: (l,) + (0,) * len(shape), pipeline_mode=pl.Buffered(1))


def _split_bf16(a):
    hi = a.astype(bf16)
    lo = (a - hi.astype(f32)).astype(bf16)
    return hi, lo


def _mod_norm(x, gain_scale, shift):
    ms = jnp.mean(x * x, axis=-1, keepdims=True)
    return (x * lax.rsqrt(ms + EPS)) * gain_scale + shift


def _mod_kernel(c_ref, w_ref, b_ref, o_ref):
    c = c_ref[...]
    sc = c * (1.0 / (1.0 + jnp.exp(-c)))
    s_hi, s_lo = _split_bf16(sc)
    w_hi, w_lo = _split_bf16(w_ref[0])
    o_ref[0] = _dot(s_hi, w_hi) + _dot(s_hi, w_lo) + _dot(s_lo, w_hi) + b_ref[0]


def _mod_call(c_all, ada_w, ada_b):
    depth, _, n = ada_w.shape
    return pl.pallas_call(
        _mod_kernel,
        grid=(depth, n // MOD_TN),
        in_specs=[
            pl.BlockSpec((MOD_ROWS, D_MODEL), lambda l, j: (0, 0)),
            pl.BlockSpec((1, D_MODEL, MOD_TN), lambda l, j: (l, 0, j)),
            pl.BlockSpec((1, 1, MOD_TN), lambda l, j: (l, 0, j)),
        ],
        out_specs=pl.BlockSpec((1, MOD_ROWS, MOD_TN), lambda l, j: (l, 0, j)),
        out_shape=jax.ShapeDtypeStruct((depth, MOD_ROWS, n), f32),
        compiler_params=_params(("arbitrary", "arbitrary")),
        name="adaln_mod",
    )(c_all, ada_w, ada_b.reshape(depth, 1, n))


def _ffn_kernel(x_ref, mod_ref, g_ref, wi_ref, wo_ref, o_ref, acc_ref, *, row0):
    shift, scale, gate = (mod_ref[0, 0, row0 + i:row0 + i + 1, :] for i in range(3))
    subs = [slice(r, r + FFN_SUB) for r in range(0, x_ref.shape[1], FFN_SUB)]
    gain_scale = g_ref[...] * (1.0 + scale)
    hbs = [_mod_norm(x_ref[0, rows, :], gain_scale, shift).astype(bf16) for rows in subs]
    for c in range(D_FF // FFN_CK):
        for rows, hb in zip(subs, hbs):
            g = _dot(hb, wi_ref[:, c * FFN_CK:(c + 1) * FFN_CK])
            u = _dot(hb, wi_ref[:, D_FF + c * FFN_CK:D_FF + (c + 1) * FFN_CK])
            a = (g * (1.0 / (1.0 + jnp.exp(-g))) * u).astype(bf16)
            part = _dot(a, wo_ref[c * FFN_CK:(c + 1) * FFN_CK, :])
            if c == 0:
                acc_ref[rows, :] = part
            else:
                acc_ref[rows, :] += part
    for rows in subs:
        o_ref[0, rows, :] = x_ref[0, rows, :] + (0.5 * gate) * acc_ref[rows, :]


def _mod_spec(l, row0):
    return pl.BlockSpec((1, 1, N_MOD, D_MODEL), lambda i, *_: (l, row0 + i, 0, 0))


def _ffn_call(x, mods, mrow, l, gain, wi, wo, row0):
    b, s, d = x.shape
    tm = min(FFN_TM, s)
    return pl.pallas_call(
        functools.partial(_ffn_kernel, row0=row0),
        grid=(b, s // tm),
        in_specs=[
            pl.BlockSpec((1, tm, d), lambda i, j: (i, j, 0)),
            _mod_spec(l, mrow),
            _layer((1, d), l),
            _layer((d, 2 * D_FF), l),
            _layer((D_FF, d), l),
        ],
        out_specs=pl.BlockSpec((1, tm, d), lambda i, j: (i, j, 0)),
        out_shape=jax.ShapeDtypeStruct(x.shape, f32),
        scratch_shapes=[pltpu.VMEM((tm, d), f32)],
        compiler_params=_params(("parallel", "parallel")),
        name="ffn",
    )(x, mods, gain, wi, wo)


def _group_rms(z, gmean_ref, gain):
    outs = []
    for c in range(QK_W // V7X_MXU_DIM):
        zc = z[:, c * V7X_MXU_DIM:(c + 1) * V7X_MXU_DIM]
        ms = _dot((zc * zc).astype(bf16), gmean_ref[...])
        outs.append(zc * lax.rsqrt(ms + EPS))
    return jnp.concatenate(outs, axis=-1) * gain


def _mixin_kernel(x_ref, mod_ref, g_ref, win_ref, gq_ref, gk_ref, gmean_ref, dc_ref, ds_ref,
                  q_ref, kt_ref, v_ref, xc_ref, xs_ref):
    shift, gain_scale = mod_ref[0, 0, 3:4, :], g_ref[...] * (1.0 + mod_ref[0, 0, 4:5, :])
    for r0 in range(0, x_ref.shape[1], MIX_SUB):
        rows = slice(r0, r0 + MIX_SUB)
        hb = _mod_norm(x_ref[0, rows, :], gain_scale, shift).astype(bf16)
        zq = _dot(hb, win_ref[:, 0:QK_W])
        q = (_group_rms(zq, gmean_ref, gq_ref[...]) * (ATTN_SCALE * LOG2E)).astype(bf16)
        first = lax.broadcasted_iota(jnp.int32, q.shape, 1) % HEAD_W < HEAD_DIM
        zero = jnp.zeros_like(q)
        q_ref[0, 0, rows, :] = jnp.where(first, q, zero)
        q_ref[1, 0, rows, :] = jnp.where(first, zero, q)
        zk = _dot(hb, win_ref[:, QK_W:2 * QK_W])
        kt_ref[0, :, rows] = _group_rms(zk, gmean_ref, gk_ref[...]).T.astype(bf16)
        v = _dot(hb, win_ref[:, 2 * QK_W:2 * QK_W + V_W]).astype(bf16)
        ones = jnp.ones((MIX_SUB, HEAD_W), bf16)
        for h in range(N_HEADS):
            v_ref[0, rows, 2 * h * HEAD_W:(2 * h + 1) * HEAD_W] = v[:, h * HEAD_W:(h + 1) * HEAD_W]
            v_ref[0, rows, (2 * h + 1) * HEAD_W:(2 * h + 2) * HEAD_W] = ones
        fb = _dot(hb, win_ref[:, 2 * QK_W + V_W:]).astype(bf16)
        for c in range(F_W // V7X_MXU_DIM):
            cols = slice(c * V7X_MXU_DIM, (c + 1) * V7X_MXU_DIM)
            xc_ref[0, rows, cols] = _dot(fb[:, cols], dc_ref[...]).astype(bf16)
            xs_ref[0, rows, cols] = _dot(fb[:, cols], ds_ref[...]).astype(bf16)


def _mixin_call(x, mods, mrow, l, gain, w_in, gq, gk, gmean, dc, ds):
    b, s, d = x.shape
    tm = min(MIX_TM, s)
    row = lambda i, j: (i, j, 0)
    tok = lambda w: jax.ShapeDtypeStruct((b, s, w), bf16)
    return pl.pallas_call(
        _mixin_kernel,
        grid=(b, s // tm),
        in_specs=[
            pl.BlockSpec((1, tm, d), row),
            _mod_spec(l, mrow),
            _layer((1, d), l),
            _layer(w_in.shape[1:], l),
            _layer((1, QK_W), l),
            _layer((1, QK_W), l),
            _resident((V7X_MXU_DIM, V7X_MXU_DIM)),
            _resident((V7X_MXU_DIM, V7X_MXU_DIM)),
            _resident((V7X_MXU_DIM, V7X_MXU_DIM)),
        ],
        out_specs=[
            pl.BlockSpec((2, 1, tm, QK_W), lambda i, j: (0, i, j, 0)),
            pl.BlockSpec((1, QK_W, tm), lambda i, j: (i, 0, j)),
            pl.BlockSpec((1, tm, 2 * V_W), row),
            pl.BlockSpec((1, tm, F_W), row),
            pl.BlockSpec((1, tm, F_W), row),
        ],
        out_shape=[jax.ShapeDtypeStruct((2, b, s, QK_W), bf16), jax.ShapeDtypeStruct((b, QK_W, s), bf16),
                   tok(2 * V_W), tok(F_W), tok(F_W)],
        compiler_params=_params(("parallel", "parallel")),
        name="mix_in",
    )(x, mods, gain, w_in, gq, gk, gmean, dc, ds)


def _attn_kernel(lqk_ref, q_ref, kt_ref, v_ref, bias_ref, o_ref, s_scr, m_scr, *, n_tiles, n_sub, lambda_init):
    tq, tk = ATT_TQ, ATT_TK
    qt = pl.program_id(2) * n_sub

    def score_map(i, carry):
        sub, j = i // 2, i % 2
        qj = q_ref[j, 0, pl.ds(pl.multiple_of(sub * tq, tq), tq), :]
        m_run = jnp.full((tq, V7X_LANES), -jnp.inf, bf16)
        for kt in range(n_tiles):
            tile = jnp.clip(kt - (qt + sub) * (tq // tk) + 2, 0, N_BIAS_TILES - 1)
            sj = _dot(qj, kt_ref[0, :, kt * tk:(kt + 1) * tk]) + bias_ref[j, tile]
            s_scr[i, :, kt * tk:(kt + 1) * tk] = sj
            m_run = jnp.maximum(m_run, jnp.maximum(sj[:, :V7X_LANES], sj[:, V7X_LANES:]).astype(bf16))
        m_scr[i] = m_run.astype(f32)
        return carry

    lax.fori_loop(0, 2 * n_sub, score_map, 0)

    lqk = lqk_ref[...]
    lam = (jnp.exp(jnp.sum(lqk[0:1] * lqk[1:2], axis=-1, keepdims=True))
           - jnp.exp(jnp.sum(lqk[2:3] * lqk[3:4], axis=-1, keepdims=True)) + lambda_init)
    for sub in range(n_sub):
        heads = []
        for i in (2 * sub, 2 * sub + 1):
            m = jnp.max(m_scr[i], axis=-1, keepdims=True)
            p = jnp.exp2((s_scr[i] - m).astype(bf16))
            ov = _dot(p, v_ref[0])
            heads.append(ov[:, :HEAD_W] / ov[:, HEAD_W:])
        o_ref[0, sub * tq:(sub + 1) * tq, :] = (heads[0] - lam * heads[1]).astype(bf16)


def _attn_call(l, lqk, q, kt, vx, bias, lambda_init):
    _, b, s, _ = q.shape
    tq, tk = ATT_TQ, ATT_TK
    assert s % tk == 0 and s % tq == 0
    n_sub = max(1, min(s // tq, ATT_LOGIT_SCRATCH_BYTES // (2 * tq * s * 4)))
    rows = n_sub * tq
    return pl.pallas_call(
        functools.partial(_attn_kernel, n_tiles=s // tk, n_sub=n_sub, lambda_init=lambda_init),
        grid=(b, N_HEADS, s // rows),
        in_specs=[
            _layer((4, HEAD_DIM), l),
            pl.BlockSpec((2, 1, rows, HEAD_W), lambda i, h, j: (0, i, j, h)),
            pl.BlockSpec((1, HEAD_W, s), lambda i, h, j: (i, h, 0)),
            pl.BlockSpec((1, s, 2 * HEAD_W), lambda i, h, j: (i, 0, h)),
            pl.BlockSpec((2, N_BIAS_TILES, tq, tk), lambda i, h, j: (h, 0, 0, 0)),
        ],
        out_specs=pl.BlockSpec((1, rows, HEAD_W), lambda i, h, j: (i, j, h)),
        out_shape=jax.ShapeDtypeStruct((b, s, V_W), bf16),
        scratch_shapes=[pltpu.VMEM((2 * n_sub, tq, s), f32), pltpu.VMEM((2 * n_sub, tq, V7X_LANES), f32)],
        compiler_params=_params(("parallel", "parallel", "arbitrary")),
        name="diff_attn",
    )(lqk, q, kt, vx, bias)


def _mixout_kernel(x_ref, mod_ref, o_ref, g_ref, xc_ref, xs_ref, cmo_ref, smo_ref, cme_ref, sme_ref, wout_ref, y_ref,
                   odd_scr, even_scr, fo_scr, *, scale, quarter, lambda_init):
    @pl.when(pl.program_id(1) == 0)
    def _():
        for t, src in enumerate((xc_ref, xs_ref)):
            x0, x1, x2, x3 = (src[0, i * quarter:(i + 1) * quarter] for i in range(4))
            odd_scr[t, :quarter] = x0 - x2
            odd_scr[t, quarter:] = x1 - x3
            lo, hi = x0 + x2, x1 + x3
            even_scr[t, 0] = lo + hi
            even_scr[t, 1] = lo - hi

    tm = cmo_ref.shape[1]
    parts = [(c, _dot(cmo_ref[i], odd_scr[0]) - _dot(smo_ref[i], odd_scr[1])) for i, c in enumerate(DFT_CLASSES[:2])]
    parts += [(c, _dot(cme_ref[i], even_scr[0, i]) - _dot(sme_ref[i], even_scr[1, i]))
              for i, c in enumerate(DFT_CLASSES[2:])]
    for c, fo in parts:
        fo = fo * scale
        for ch in range(F_W // V7X_LANES):
            fo_scr[ch, pl.ds(c, tm, stride=4), :] = fo[:, ch * V7X_LANES:(ch + 1) * V7X_LANES]
    fo = jnp.concatenate([fo_scr[ch] for ch in range(F_W // V7X_LANES)], axis=-1).astype(bf16)
    heads = []
    for h in range(N_HEADS):
        oh = o_ref[0, :, h * HEAD_W:(h + 1) * HEAD_W].astype(f32)
        ms = jnp.mean(oh * oh, axis=-1, keepdims=True)
        heads.append(((oh * lax.rsqrt(ms + EPS) * g_ref[...]) * (1.0 - lambda_init)).astype(bf16))
    mix = _dot(jnp.concatenate(heads, axis=-1), wout_ref[0:V_W, :]) + _dot(fo, wout_ref[V_W:, :])
    y_ref[0] = x_ref[0] + mod_ref[0, 0, 5:6, :] * mix


def _mixout_call(x, mods, mrow, l, o, subln, xc, xs, cm, sm, w_out, lambda_init):
    b, s, d = x.shape
    quarter = s // 4
    tm = min(OUT_TM, quarter)
    row = lambda i, j: (i, j, 0)
    seq_buffers = 2 if 2 * 2 * s * F_W * 2 <= MIXOUT_SEQ_INPUT_BYTES else 1
    seq = pl.BlockSpec((1, s, F_W), lambda i, j: (i, 0, 0), pipeline_mode=pl.Buffered(seq_buffers))
    odd = pl.BlockSpec((2, tm, 2 * quarter), lambda i, j: (0, j, 0))
    even = pl.BlockSpec((2, tm, quarter), lambda i, j: (1, j, 0))
    return pl.pallas_call(
        functools.partial(_mixout_kernel, scale=1.0 / math.sqrt(F_GROUP * s), quarter=quarter,
                          lambda_init=lambda_init),
        grid=(b, quarter // tm),
        in_specs=[
            pl.BlockSpec((1, 4 * tm, d), row),
            _mod_spec(l, mrow),
            pl.BlockSpec((1, 4 * tm, V_W), row),
            _layer((1, HEAD_W), l),
            seq, seq, odd, odd, even, even,
            _layer(w_out.shape[1:], l),
        ],
        out_specs=pl.BlockSpec((1, 4 * tm, d), row),
        out_shape=jax.ShapeDtypeStruct(x.shape, f32),
        scratch_shapes=[pltpu.VMEM((2, 2 * quarter, F_W), bf16), pltpu.VMEM((2, 2, quarter, F_W), bf16),
                        pltpu.VMEM((F_W // V7X_LANES, 4 * tm, V7X_LANES), f32)],
        compiler_params=_params(("parallel", "arbitrary")),
        name="mix_out",
    )(x, mods, o, subln, xc, xs, cm, sm, cm, sm, w_out)


def _dft_kernel(ac_ref, as_ref, bc_ref, bs_ref, cm_ref, sm_ref):
    bc, bsn = bc_ref[0], bs_ref[0]
    for i in range(ac_ref.shape[0]):
        ac, asn = ac_ref[i], as_ref[i]
        cm_ref[0, i * DFT_B:(i + 1) * DFT_B, :] = (ac * bc - asn * bsn).astype(bf16)
        sm_ref[0, i * DFT_B:(i + 1) * DFT_B, :] = (asn * bc + ac * bsn).astype(bf16)


def _dft_tables(s):
    half, quarter = s // 2, s // 4
    pos = np.arange(half, dtype=np.int64)
    ang = lambda freq: 2.0 * np.pi * ((freq[..., None] * pos) % s).astype(np.float64) / s
    a = ang(np.arange(quarter // DFT_B, dtype=np.int64) * (4 * DFT_B))[:, None, :]
    b = ang(4 * np.arange(DFT_B, dtype=np.int64)[None, :] + np.asarray(DFT_CLASSES, dtype=np.int64)[:, None])
    tab = lambda m: jnp.asarray(m.astype(np.float32))
    return tab(np.cos(a)), tab(np.sin(a)), tab(np.cos(b)), tab(np.sin(b))


def _dft_call(s):
    half, quarter = s // 2, s // 4
    a_per_step = min(DFT_A_PER_STEP, quarter // DFT_B)
    rows = DFT_B * a_per_step
    assert quarter % rows == 0
    ac, asn, bc, bsn = _dft_tables(s)
    aspec = pl.BlockSpec((a_per_step, 1, half), lambda c, a: (a, 0, 0))
    bspec = pl.BlockSpec((1, DFT_B, half), lambda c, a: (c, 0, 0))
    out = pl.BlockSpec((1, rows, half), lambda c, a: (c, a, 0))
    return pl.pallas_call(
        _dft_kernel,
        grid=(len(DFT_CLASSES), quarter // rows),
        in_specs=[aspec, aspec, bspec, bspec],
        out_specs=[out, out],
        out_shape=[jax.ShapeDtypeStruct((len(DFT_CLASSES), quarter, half), bf16)] * 2,
        compiler_params=_params(("parallel", "parallel")),
        name="dft_matrix",
    )(ac, asn, bc, bsn)


def _rel_bucket_np(rel):
    nb = NUM_BUCKETS // 2
    max_exact = nb // 2
    ret = (rel > 0).astype(np.int32) * nb
    n = np.abs(rel)
    nf = np.maximum(n, 1).astype(np.float32)
    ratio = np.log(nf / np.float32(max_exact)) / np.float32(math.log(REL_MAX_DISTANCE / max_exact))
    large = max_exact + (ratio * np.float32(nb - max_exact)).astype(np.int32)
    large = np.minimum(large, nb - 1)
    return (ret + np.where(n < max_exact, n, large)).astype(np.int32)


def _bucket_tiles():
    qpos = np.arange(ATT_TQ)[:, None]
    kpos = np.arange(ATT_TK)[None, :]
    return np.stack([_rel_bucket_np((d * ATT_TK + kpos) - qpos) for d in range(-2, N_BIAS_TILES - 2)])


def _bias_kernel(table_ref, bucket_ref, o_ref):
    n_maps = o_ref.shape[0]

    def rows(r, carry):
        r0 = pl.multiple_of(r * BIAS_ROWS, BIAS_ROWS)
        for c in range(ATT_TK // V7X_LANES):
            cols = slice(c * V7X_LANES, (c + 1) * V7X_LANES)
            bucket = bucket_ref[0, pl.ds(r0, BIAS_ROWS), cols]
            for hj in range(n_maps):
                row = jnp.broadcast_to(table_ref[hj:hj + 1, :], bucket.shape)
                o_ref[hj, 0, pl.ds(r0, BIAS_ROWS), cols] = jnp.take_along_axis(row, bucket, axis=1) * LOG2E
        return carry

    lax.fori_loop(0, ATT_TQ // BIAS_ROWS, rows, 0)


def _bias_call(rel_bias):
    n_maps = 2 * N_HEADS
    table = jnp.pad(rel_bias.reshape(NUM_BUCKETS, n_maps).T, ((0, 0), (0, V7X_LANES - NUM_BUCKETS)))
    return pl.pallas_call(
        _bias_kernel,
        grid=(N_BIAS_TILES,),
        in_specs=[
            _resident((n_maps, V7X_LANES)),
            pl.BlockSpec((1, ATT_TQ, ATT_TK), lambda t: (t, 0, 0)),
        ],
        out_specs=pl.BlockSpec((n_maps, 1, ATT_TQ, ATT_TK), lambda t: (0, t, 0, 0)),
        out_shape=jax.ShapeDtypeStruct((n_maps, N_BIAS_TILES, ATT_TQ, ATT_TK), f32),
        compiler_params=_params(("arbitrary",)),
        name="rel_bias_tiles",
    )(table, jnp.asarray(_bucket_tiles()))


def _channel_dft_tables():
    idx = np.arange(F_GROUP)
    ang = 2.0 * np.pi * ((idx[:, None] * idx[None, :]) % F_GROUP) / F_GROUP
    eye = np.eye(V7X_MXU_DIM // F_GROUP)
    return (jnp.asarray(np.kron(eye, np.cos(ang)), dtype=bf16), jnp.asarray(np.kron(eye, np.sin(ang)), dtype=bf16))


def _group_mean_table():
    block = np.full((HEAD_DIM, HEAD_DIM), 1.0 / HEAD_DIM)
    return jnp.asarray(np.kron(np.eye(V7X_MXU_DIM // HEAD_DIM), block), dtype=bf16)


def _trunk(x, mods, mrow, p, bias, cm, sm, consts):
    gmean, dc, ds = consts
    depth = p["w_in"].shape[0]
    for l in range(depth):
        lambda_init = 0.8 - 0.6 * math.exp(-0.3 * l)
        x = _ffn_call(x, mods, mrow, l, p["norm_ffn1"], p["ffn1_wi"], p["ffn1_wo"], 0)
        q, kt, vx, xc, xs = _mixin_call(x, mods, mrow, l, p["norm_mix"], p["w_in"], p["q_norm"], p["k_norm"],
                                        gmean, dc, ds)
        o = _attn_call(l, p["lambda_qk"], q, kt, vx, bias, lambda_init)
        x = _mixout_call(x, mods, mrow, l, o, p["subln"], xc, xs, cm, sm, p["w_out"], lambda_init)
        x = _ffn_call(x, mods, mrow, l, p["norm_ffn2"], p["ffn2_wi"], p["ffn2_wo"], 6)
    return x


def kernel(x_prompt, x_sample, c_prompt, c_sample, ada_w, ada_b, norm_ffn1, norm_mix, norm_ffn2, ffn1_wi, ffn1_wo,
           ffn2_wi, ffn2_wo, w_in, w_out, q_norm, k_norm, lambda_qk, subln, rel_bias):
    depth = ada_w.shape[0]
    nb_p, nb_s = c_prompt.shape[0], c_sample.shape[0]
    assert nb_p + nb_s <= MOD_ROWS
    c_all = jnp.concatenate([c_prompt, c_sample, jnp.zeros((MOD_ROWS - nb_p - nb_s, D_MODEL), f32)], axis=0)
    mods = _mod_call(c_all, ada_w, ada_b).reshape(depth, MOD_ROWS, N_MOD, D_MODEL)

    row = lambda a: a.reshape(depth, 1, -1)
    tile_heads = lambda a: jnp.tile(a, (1, QK_W // HEAD_DIM)).reshape(depth, 1, QK_W)
    p = dict(
        norm_ffn1=row(norm_ffn1), norm_mix=row(norm_mix), norm_ffn2=row(norm_ffn2),
        ffn1_wi=ffn1_wi.astype(bf16), ffn1_wo=ffn1_wo.astype(bf16),
        ffn2_wi=ffn2_wi.astype(bf16), ffn2_wo=ffn2_wo.astype(bf16),
        w_in=w_in.astype(bf16), w_out=w_out.astype(bf16),
        q_norm=tile_heads(q_norm), k_norm=tile_heads(k_norm),
        lambda_qk=lambda_qk, subln=row(subln),
    )
    bias = _bias_call(rel_bias)
    consts = (_group_mean_table(),) + _channel_dft_tables()

    outs = []
    for x, mrow in ((x_prompt, 0), (x_sample, nb_p)):
        cm, sm = _dft_call(x.shape[1])
        outs.append(_trunk(x, mods, mrow, p, bias, cm, sm, consts))
    return tuple(outs)
```

```python
import functools
import math

import numpy as np
import jax
import jax.numpy as jnp
from jax import lax
from jax.experimental import pallas as pl
from jax.experimental.pallas import tpu as pltpu

D_MODEL = 1024
HEAD_DIM = 64
N_HEADS = 4
HEAD_W = 2 * HEAD_DIM
QK_W = N_HEADS * HEAD_W
V_W = QK_W
F_W = 512
F_GROUP = 64
D_FF = 2816
N_MOD = 9
NUM_BUCKETS = 32
REL_MAX_DISTANCE = 128
EPS = 1e-6
ATTN_SCALE = HEAD_DIM ** -0.5

V7X_VMEM_LIMIT_BYTES = 58 * 1024 * 1024
V7X_LANES = 128
V7X_MXU_DIM = 256
MOD_ROWS = 16
MOD_TN = 2304
FFN_TM = 1024
FFN_SUB = 512
FFN_CK = V7X_MXU_DIM
MIX_TM = 1024
MIX_SUB = 512
ATT_TQ = 512
ATT_TK = V7X_MXU_DIM
ATT_LOGIT_SCRATCH_BYTES = 32 * 1024 * 1024
N_BIAS_TILES = ATT_TQ // ATT_TK + 4
BIAS_ROWS = 128
LOG2E = math.log2(math.e)
OUT_TM = V7X_MXU_DIM
MIXOUT_SEQ_INPUT_BYTES = 8 * 1024 * 1024
DFT_B = 64
DFT_A_PER_STEP = 4
DFT_CLASSES = (1, 3, 0, 2)

f32 = jnp.float32
bf16 = jnp.bfloat16


def _dot(a, b):
    return jnp.dot(a, b, preferred_element_type=f32)


def _params(sem, vmem=V7X_VMEM_LIMIT_BYTES):
    return pltpu.CompilerParams(dimension_semantics=sem, vmem_limit_bytes=vmem)


def _resident(shape):
    return pl.BlockSpec(shape, lambda *_: (0,) * len(shape), pipeline_mode=pl.Buffered(1))


def _layer(shape, l):
    return pl.BlockSpec((None,) + tuple(shape), lambda *_: (l,) + (0,) * len(shape), pipeline_mode=pl.Buffered(1))


def _split_bf16(a):
    hi = a.astype(bf16)
    lo = (a - hi.astype(f32)).astype(bf16)
    return hi, lo


def _mod_norm(x, gain_scale, shift):
    ms = jnp.mean(x * x, axis=-1, keepdims=True)
    return (x * lax.rsqrt(ms + EPS)) * gain_scale + shift


def _mod_kernel(c_ref, w_ref, b_ref, o_ref):
    c = c_ref[...]
    sc = c * (1.0 / (1.0 + jnp.exp(-c)))
    s_hi, s_lo = _split_bf16(sc)
    w_hi, w_lo = _split_bf16(w_ref[0])
    o_ref[0] = _dot(s_hi, w_hi) + _dot(s_hi, w_lo) + _dot(s_lo, w_hi) + b_ref[0]


def _mod_call(c_all, ada_w, ada_b):
    depth, _, n = ada_w.shape
    return pl.pallas_call(
        _mod_kernel,
        grid=(depth, n // MOD_TN),
        in_specs=[
            pl.BlockSpec((MOD_ROWS, D_MODEL), lambda l, j: (0, 0)),
            pl.BlockSpec((1, D_MODEL, MOD_TN), lambda l, j: (l, 0, j)),
            pl.BlockSpec((1, 1, MOD_TN), lambda l, j: (l, 0, j)),
        ],
        out_specs=pl.BlockSpec((1, MOD_ROWS, MOD_TN), lambda l, j: (l, 0, j)),
        out_shape=jax.ShapeDtypeStruct((depth, MOD_ROWS, n), f32),
        compiler_params=_params(("arbitrary", "arbitrary")),
        name="adaln_mod",
    )(c_all, ada_w, ada_b.reshape(depth, 1, n))


def _ffn_kernel(x_ref, mod_ref, g_ref, wi_ref, wo_ref, o_ref, acc_ref, *, row0):
    shift, scale, gate = (mod_ref[0, 0, row0 + i:row0 + i + 1, :] for i in range(3))
    subs = [slice(r, r + FFN_SUB) for r in range(0, x_ref.shape[1], FFN_SUB)]
    gain_scale = g_ref[...] * (1.0 + scale)
    hbs = [_mod_norm(x_ref[0, rows, :], gain_scale, shift).astype(bf16) for rows in subs]
    for c in range(D_FF // FFN_CK):
        for rows, hb in zip(subs, hbs):
            g = _dot(hb, wi_ref[:, c * FFN_CK:(c + 1) * FFN_CK])
            u = _dot(hb, wi_ref[:, D_FF + c * FFN_CK:D_FF + (c + 1) * FFN_CK])
            a = (g * (1.0 / (1.0 + jnp.exp(-g))) * u).astype(bf16)
            part = _dot(a, wo_ref[c * FFN_CK:(c + 1) * FFN_CK, :])
            if c == 0:
                acc_ref[rows, :] = part
            else:
                acc_ref[rows, :] += part
    for rows in subs:
        o_ref[0, rows, :] = x_ref[0, rows, :] + (0.5 * gate) * acc_ref[rows, :]


def _mod_spec(l, row0):
    return pl.BlockSpec((1, 1, N_MOD, D_MODEL), lambda i, *_: (l, row0 + i, 0, 0))


def _ffn_call(x, mods, mrow, l, gain, wi, wo, row0):
    b, s, d = x.shape
    tm = min(FFN_TM, s)
    return pl.pallas_call(
        functools.partial(_ffn_kernel, row0=row0),
        grid=(b, s // tm),
        in_specs=[
            pl.BlockSpec((1, tm, d), lambda i, j: (i, j, 0)),
            _mod_spec(l, mrow),
            _layer((1, d), l),
            _layer((d, 2 * D_FF), l),
            _layer((D_FF, d), l),
        ],
        out_specs=pl.BlockSpec((1, tm, d), lambda i, j: (i, j, 0)),
        out_shape=jax.ShapeDtypeStruct(x.shape, f32),
        scratch_shapes=[pltpu.VMEM((tm, d), f32)],
        compiler_params=_params(("parallel", "parallel")),
        name="ffn",
    )(x, mods, gain, wi, wo)


def _group_rms(z, gmean_ref, gain):
    outs = []
    for c in range(QK_W // V7X_MXU_DIM):
        zc = z[:, c * V7X_MXU_DIM:(c + 1) * V7X_MXU_DIM]
        ms = _dot((zc * zc).astype(bf16), gmean_ref[...])
        outs.append(zc * lax.rsqrt(ms + EPS))
    return jnp.concatenate(outs, axis=-1) * gain


def _mixin_kernel(x_ref, mod_ref, g_ref, win_ref, gq_ref, gk_ref, gmean_ref, dc_ref, ds_ref,
                  q_ref, kt_ref, v_ref, xc_ref, xs_ref):
    shift, gain_scale = mod_ref[0, 0, 3:4, :], g_ref[...] * (1.0 + mod_ref[0, 0, 4:5, :])
    for r0 in range(0, x_ref.shape[1], MIX_SUB):
        rows = slice(r0, r0 + MIX_SUB)
        hb = _mod_norm(x_ref[0, rows, :], gain_scale, shift).astype(bf16)
        zq = _dot(hb, win_ref[:, 0:QK_W])
        q = (_group_rms(zq, gmean_ref, gq_ref[...]) * (ATTN_SCALE * LOG2E)).astype(bf16)
        first = lax.broadcasted_iota(jnp.int32, q.shape, 1) % HEAD_W < HEAD_DIM
        zero = jnp.zeros_like(q)
        q_ref[0, 0, rows, :] = jnp.where(first, q, zero)
        q_ref[1, 0, rows, :] = jnp.where(first, zero, q)
        zk = _dot(hb, win_ref[:, QK_W:2 * QK_W])
        kt_ref[0, :, rows] = _group_rms(zk, gmean_ref, gk_ref[...]).T.astype(bf16)
        v = _dot(hb, win_ref[:, 2 * QK_W:2 * QK_W + V_W]).astype(bf16)
        ones = jnp.ones((MIX_SUB, HEAD_W), bf16)
        for h in range(N_HEADS):
            v_ref[0, rows, 2 * h * HEAD_W:(2 * h + 1) * HEAD_W] = v[:, h * HEAD_W:(h + 1) * HEAD_W]
            v_ref[0, rows, (2 * h + 1) * HEAD_W:(2 * h + 2) * HEAD_W] = ones
        fb = _dot(hb, win_ref[:, 2 * QK_W + V_W:]).astype(bf16)
        for c in range(F_W // V7X_MXU_DIM):
            cols = slice(c * V7X_MXU_DIM, (c + 1) * V7X_MXU_DIM)
            xc_ref[0, rows, cols] = _dot(fb[:, cols], dc_ref[...]).astype(bf16)
            xs_ref[0, rows, cols] = _dot(fb[:, cols], ds_ref[...]).astype(bf16)


def _mixin_call(x, mods, mrow, l, gain, w_in, gq, gk, gmean, dc, ds):
    b, s, d = x.shape
    tm = min(MIX_TM, s)
    row = lambda i, j: (i, j, 0)
    tok = lambda w: jax.ShapeDtypeStruct((b, s, w), bf16)
    return pl.pallas_call(
        _mixin_kernel,
        grid=(b, s // tm),
        in_specs=[
            pl.BlockSpec((1, tm, d), row),
            _mod_spec(l, mrow),
            _layer((1, d), l),
            _layer(w_in.shape[1:], l),
            _layer((1, QK_W), l),
            _layer((1, QK_W), l),
            _resident((V7X_MXU_DIM, V7X_MXU_DIM)),
            _resident((V7X_MXU_DIM, V7X_MXU_DIM)),
            _resident((V7X_MXU_DIM, V7X_MXU_DIM)),
        ],
        out_specs=[
            pl.BlockSpec((2, 1, tm, QK_W), lambda i, j: (0, i, j, 0)),
            pl.BlockSpec((1, QK_W, tm), lambda i, j: (i, 0, j)),
            pl.BlockSpec((1, tm, 2 * V_W), row),
            pl.BlockSpec((1, tm, F_W), row),
            pl.BlockSpec((1, tm, F_W), row),
        ],
        out_shape=[jax.ShapeDtypeStruct((2, b, s, QK_W), bf16), jax.ShapeDtypeStruct((b, QK_W, s), bf16),
                   tok(2 * V_W), tok(F_W), tok(F_W)],
        compiler_params=_params(("parallel", "parallel")),
        name="mix_in",
    )(x, mods, gain, w_in, gq, gk, gmean, dc, ds)


def _attn_kernel(lqk_ref, q_ref, kt_ref, v_ref, bias_ref, o_ref, s_scr, m_scr, *, n_tiles, n_sub, lambda_init):
    tq, tk = ATT_TQ, ATT_TK
    qt = pl.program_id(2) * n_sub

    def score_map(i, carry):
        sub, j = i // 2, i % 2
        qj = q_ref[j, 0, pl.ds(pl.multiple_of(sub * tq, tq), tq), :]
        m_run = jnp.full((tq, V7X_LANES), -jnp.inf, bf16)
        for kt in range(n_tiles):
            tile = jnp.clip(kt - (qt + sub) * (tq // tk) + 2, 0, N_BIAS_TILES - 1)
            sj = _dot(qj, kt_ref[0, :, kt * tk:(kt + 1) * tk]) + bias_ref[j, tile]
            s_scr[i, :, kt * tk:(kt + 1) * tk] = sj
            m_run = jnp.maximum(m_run, jnp.maximum(sj[:, :V7X_LANES], sj[:, V7X_LANES:]).astype(bf16))
        m_scr[i] = m_run.astype(f32)
        return carry

    lax.fori_loop(0, 2 * n_sub, score_map, 0)

    lqk = lqk_ref[...]
    lam = (jnp.exp(jnp.sum(lqk[0:1] * lqk[1:2], axis=-1, keepdims=True))
           - jnp.exp(jnp.sum(lqk[2:3] * lqk[3:4], axis=-1, keepdims=True)) + lambda_init)
    for sub in range(n_sub):
        heads = []
        for i in (2 * sub, 2 * sub + 1):
            m = jnp.max(m_scr[i], axis=-1, keepdims=True)
            p = jnp.exp2((s_scr[i] - m).astype(bf16))
            ov = _dot(p, v_ref[0])
            heads.append(ov[:, :HEAD_W] / ov[:, HEAD_W:])
        o_ref[0, sub * tq:(sub + 1) * tq, :] = (heads[0] - lam * heads[1]).astype(bf16)


def _attn_call(l, lqk, q, kt, vx, bias, lambda_init):
    _, b, s, _ = q.shape
    tq, tk = ATT_TQ, ATT_TK
    assert s % tk == 0 and s % tq == 0
    n_sub = max(1, min(s // tq, ATT_LOGIT_SCRATCH_BYTES // (2 * tq * s * 4)))
    rows = n_sub * tq
    return pl.pallas_call(
        functools.partial(_attn_kernel, n_tiles=s // tk, n_sub=n_sub, lambda_init=lambda_init),
        grid=(b, N_HEADS, s // rows),
        in_specs=[
            _layer((4, HEAD_DIM), l),
            pl.BlockSpec((2, 1, rows, HEAD_W), lambda i, h, j: (0, i, j, h)),
            pl.BlockSpec((1, HEAD_W, s), lambda i, h, j: (i, h, 0)),
            pl.BlockSpec((1, s, 2 * HEAD_W), lambda i, h, j: (i, 0, h)),
            pl.BlockSpec((2, N_BIAS_TILES, tq, tk), lambda i, h, j: (h, 0, 0, 0)),
        ],
        out_specs=pl.BlockSpec((1, rows, HEAD_W), lambda i, h, j: (i, j, h)),
        out_shape=jax.ShapeDtypeStruct((b, s, V_W), bf16),
        scratch_shapes=[pltpu.VMEM((2 * n_sub, tq, s), f32), pltpu.VMEM((2 * n_sub, tq, V7X_LANES), f32)],
        compiler_params=_params(("parallel", "parallel", "arbitrary")),
        name="diff_attn",
    )(lqk, q, kt, vx, bias)


def _mixout_kernel(x_ref, mod_ref, o_ref, g_ref, xc_ref, xs_ref, cmo_ref, smo_ref, cme_ref, sme_ref, wout_ref, y_ref,
                   odd_scr, even_scr, fo_scr, *, scale, quarter, lambda_init):
    @pl.when(pl.program_id(1) == 0)
    def _():
        for t, src in enumerate((xc_ref, xs_ref)):
            x0, x1, x2, x3 = (src[0, i * quarter:(i + 1) * quarter] for i in range(4))
            odd_scr[t, :quarter] = x0 - x2
            odd_scr[t, quarter:] = x1 - x3
            lo, hi = x0 + x2, x1 + x3
            even_scr[t, 0] = lo + hi
            even_scr[t, 1] = lo - hi

    tm = cmo_ref.shape[1]
    parts = [(c, _dot(cmo_ref[i], odd_scr[0]) - _dot(smo_ref[i], odd_scr[1])) for i, c in enumerate(DFT_CLASSES[:2])]
    parts += [(c, _dot(cme_ref[i], even_scr[0, i]) - _dot(sme_ref[i], even_scr[1, i]))
              for i, c in enumerate(DFT_CLASSES[2:])]
    for c, fo in parts:
        fo = fo * scale
        for ch in range(F_W // V7X_LANES):
            fo_scr[ch, pl.ds(c, tm, stride=4), :] = fo[:, ch * V7X_LANES:(ch + 1) * V7X_LANES]
    fo = jnp.concatenate([fo_scr[ch] for ch in range(F_W // V7X_LANES)], axis=-1).astype(bf16)
    heads = []
    for h in range(N_HEADS):
        oh = o_ref[0, :, h * HEAD_W:(h + 1) * HEAD_W].astype(f32)
        ms = jnp.mean(oh * oh, axis=-1, keepdims=True)
        heads.append(((oh * lax.rsqrt(ms + EPS) * g_ref[...]) * (1.0 - lambda_init)).astype(bf16))
    mix = _dot(jnp.concatenate(heads, axis=-1), wout_ref[0:V_W, :]) + _dot(fo, wout_ref[V_W:, :])
    y_ref[0] = x_ref[0] + mod_ref[0, 0, 5:6, :] * mix


def _mixout_call(x, mods, mrow, l, o, subln, xc, xs, cm, sm, w_out, lambda_init):
    b, s, d = x.shape
    quarter = s // 4
    tm = min(OUT_TM, quarter)
    row = lambda i, j: (i, j, 0)
    seq_buffers = 2 if 2 * 2 * s * F_W * 2 <= MIXOUT_SEQ_INPUT_BYTES else 1
    seq = pl.BlockSpec((1, s, F_W), lambda i, j: (i, 0, 0), pipeline_mode=pl.Buffered(seq_buffers))
    odd = pl.BlockSpec((2, tm, 2 * quarter), lambda i, j: (0, j, 0))
    even = pl.BlockSpec((2, tm, quarter), lambda i, j: (1, j, 0))
    return pl.pallas_call(
        functools.partial(_mixout_kernel, scale=1.0 / math.sqrt(F_GROUP * s), quarter=quarter,
                          lambda_init=lambda_init),
        grid=(b, quarter // tm),
        in_specs=[
            pl.BlockSpec((1, 4 * tm, d), row),
            _mod_spec(l, mrow),
            pl.BlockSpec((1, 4 * tm, V_W), row),
            _layer((1, HEAD_W), l),
            seq, seq, odd, odd, even, even,
            _layer(w_out.shape[1:], l),
        ],
        out_specs=pl.BlockSpec((1, 4 * tm, d), row),
        out_shape=jax.ShapeDtypeStruct(x.shape, f32),
        scratch_shapes=[pltpu.VMEM((2, 2 * quarter, F_W), bf16), pltpu.VMEM((2, 2, quarter, F_W), bf16),
                        pltpu.VMEM((F_W // V7X_LANES, 4 * tm, V7X_LANES), f32)],
        compiler_params=_params(("parallel", "arbitrary")),
        name="mix_out",
    )(x, mods, o, subln, xc, xs, cm, sm, cm, sm, w_out)


def _dft_kernel(ac_ref, as_ref, bc_ref, bs_ref, cm_ref, sm_ref):
    bc, bsn = bc_ref[0], bs_ref[0]
    for i in range(ac_ref.shape[0]):
        ac, asn = ac_ref[i], as_ref[i]
        cm_ref[0, i * DFT_B:(i + 1) * DFT_B, :] = (ac * bc - asn * bsn).astype(bf16)
        sm_ref[0, i * DFT_B:(i + 1) * DFT_B, :] = (asn * bc + ac * bsn).astype(bf16)


def _dft_tables(s):
    half, quarter = s // 2, s // 4
    pos = np.arange(half, dtype=np.int64)
    ang = lambda freq: 2.0 * np.pi * ((freq[..., None] * pos) % s).astype(np.float64) / s
    a = ang(np.arange(quarter // DFT_B, dtype=np.int64) * (4 * DFT_B))[:, None, :]
    b = ang(4 * np.arange(DFT_B, dtype=np.int64)[None, :] + np.asarray(DFT_CLASSES, dtype=np.int64)[:, None])
    tab = lambda m: jnp.asarray(m.astype(np.float32))
    return tab(np.cos(a)), tab(np.sin(a)), tab(np.cos(b)), tab(np.sin(b))


def _dft_call(s):
    half, quarter = s // 2, s // 4
    a_per_step = min(DFT_A_PER_STEP, quarter // DFT_B)
    rows = DFT_B * a_per_step
    assert quarter % rows == 0
    ac, asn, bc, bsn = _dft_tables(s)
    aspec = pl.BlockSpec((a_per_step, 1, half), lambda c, a: (a, 0, 0))
    bspec = pl.BlockSpec((1, DFT_B, half), lambda c, a: (c, 0, 0))
    out = pl.BlockSpec((1, rows, half), lambda c, a: (c, a, 0))
    return pl.pallas_call(
        _dft_kernel,
        grid=(len(DFT_CLASSES), quarter // rows),
        in_specs=[aspec, aspec, bspec, bspec],
        out_specs=[out, out],
        out_shape=[jax.ShapeDtypeStruct((len(DFT_CLASSES), quarter, half), bf16)] * 2,
        compiler_params=_params(("parallel", "parallel")),
        name="dft_matrix",
    )(ac, asn, bc, bsn)


def _rel_bucket_np(rel):
    nb = NUM_BUCKETS // 2
    max_exact = nb // 2
    ret = (rel > 0).astype(np.int32) * nb
    n = np.abs(rel)
    nf = np.maximum(n, 1).astype(np.float32)
    ratio = np.log(nf / np.float32(max_exact)) / np.float32(math.log(REL_MAX_DISTANCE / max_exact))
    large = max_exact + (ratio * np.float32(nb - max_exact)).astype(np.int32)
    large = np.minimum(large, nb - 1)
    return (ret + np.where(n < max_exact, n, large)).astype(np.int32)


def _bucket_tiles():
    qpos = np.arange(ATT_TQ)[:, None]
    kpos = np.arange(ATT_TK)[None, :]
    return np.stack([_rel_bucket_np((d * ATT_TK + kpos) - qpos) for d in range(-2, N_BIAS_TILES - 2)])


def _bias_kernel(table_ref, bucket_ref, o_ref):
    n_maps = o_ref.shape[0]

    def rows(r, carry):
        r0 = pl.multiple_of(r * BIAS_ROWS, BIAS_ROWS)
        for c in range(ATT_TK // V7X_LANES):
            cols = slice(c * V7X_LANES, (c + 1) * V7X_LANES)
            bucket = bucket_ref[0, pl.ds(r0, BIAS_ROWS), cols]
            for hj in range(n_maps):
                row = jnp.broadcast_to(table_ref[hj:hj + 1, :], bucket.shape)
                o_ref[hj, 0, pl.ds(r0, BIAS_ROWS), cols] = jnp.take_along_axis(row, bucket, axis=1) * LOG2E
        return carry

    lax.fori_loop(0, ATT_TQ // BIAS_ROWS, rows, 0)


def _bias_call(rel_bias):
    n_maps = 2 * N_HEADS
    table = jnp.pad(rel_bias.reshape(NUM_BUCKETS, n_maps).T, ((0, 0), (0, V7X_LANES - NUM_BUCKETS)))
    return pl.pallas_call(
        _bias_kernel,
        grid=(N_BIAS_TILES,),
        in_specs=[
            _resident((n_maps, V7X_LANES)),
            pl.BlockSpec((1, ATT_TQ, ATT_TK), lambda t: (t, 0, 0)),
        ],
        out_specs=pl.BlockSpec((n_maps, 1, ATT_TQ, ATT_TK), lambda t: (0, t, 0, 0)),
        out_shape=jax.ShapeDtypeStruct((n_maps, N_BIAS_TILES, ATT_TQ, ATT_TK), f32),
        compiler_params=_params(("arbitrary",)),
        name="rel_bias_tiles",
    )(table, jnp.asarray(_bucket_tiles()))


def _channel_dft_tables():
    idx = np.arange(F_GROUP)
    ang = 2.0 * np.pi * ((idx[:, None] * idx[None, :]) % F_GROUP) / F_GROUP
    eye = np.eye(V7X_MXU_DIM // F_GROUP)
    return (jnp.asarray(np.kron(eye, np.cos(ang)), dtype=bf16), jnp.asarray(np.kron(eye, np.sin(ang)), dtype=bf16))


def _group_mean_table():
    block = np.full((HEAD_DIM, HEAD_DIM), 1.0 / HEAD_DIM)
    return jnp.asarray(np.kron(np.eye(V7X_MXU_DIM // HEAD_DIM), block), dtype=bf16)


def _trunk(x, mods, mrow, p, bias, cm, sm, consts):
    gmean, dc, ds = consts
    depth = p["w_in"].shape[0]
    for l in range(depth):
        lambda_init = 0.8 - 0.6 * math.exp(-0.3 * l)
        x = _ffn_call(x, mods, mrow, l, p["norm_ffn1"], p["ffn1_wi"], p["ffn1_wo"], 0)
        q, kt, vx, xc, xs = _mixin_call(x, mods, mrow, l, p["norm_mix"], p["w_in"], p["q_norm"], p["k_norm"],
                                        gmean, dc, ds)
        o = _attn_call(l, p["lambda_qk"], q, kt, vx, bias, lambda_init)
        x = _mixout_call(x, mods, mrow, l, o, p["subln"], xc, xs, cm, sm, p["w_out"], lambda_init)
        x = _ffn_call(x, mods, mrow, l, p["norm_ffn2"], p["ffn2_wi"], p["ffn2_wo"], 6)
    return x


def kernel(x_prompt, x_sample, c_prompt, c_sample, ada_w, ada_b, norm_ffn1, norm_mix, norm_ffn2, ffn1_wi, ffn1_wo,
           ffn2_wi, ffn2_wo, w_in, w_out, q_norm, k_norm, lambda_qk, subln, rel_bias):
    depth = ada_w.shape[0]
    nb_p, nb_s = c_prompt.shape[0], c_sample.shape[0]
    assert nb_p + nb_s <= MOD_ROWS
    c_all = jnp.concatenate([c_prompt, c_sample, jnp.zeros((MOD_ROWS - nb_p - nb_s, D_MODEL), f32)], axis=0)
    mods = _mod_call(c_all, ada_w, ada_b).reshape(depth, MOD_ROWS, N_MOD, D_MODEL)

    row = lambda a: a.reshape(depth, 1, -1)
    tile_heads = lambda a: jnp.tile(a, (1, QK_W // HEAD_DIM)).reshape(depth, 1, QK_W)
    p = dict(
        norm_ffn1=row(norm_ffn1), norm_mix=row(norm_mix), norm_ffn2=row(norm_ffn2),
        ffn1_wi=ffn1_wi.astype(bf16), ffn1_wo=ffn1_wo.astype(bf16),
        ffn2_wi=ffn2_wi.astype(bf16), ffn2_wo=ffn2_wo.astype(bf16),
        w_in=w_in.astype(bf16), w_out=w_out.astype(bf16),
        q_norm=tile_heads(q_norm), k_norm=tile_heads(k_norm),
        lambda_qk=lambda_qk, subln=row(subln),
    )
    bias = _bias_call(rel_bias)
    consts = (_group_mean_table(),) + _channel_dft_tables()

    outs = []
    for x, mrow in ((x_prompt, 0), (x_sample, nb_p)):
        cm, sm = _dft_call(x.shape[1])
        outs.append(_trunk(x, mods, mrow, p, bias, cm, sm, consts))
    return tuple(outs)
```

```python
import functools
import math

import numpy as np
import jax
import jax.numpy as jnp
from jax import lax
from jax.experimental import pallas as pl
from jax.experimental.pallas import tpu as pltpu

D_MODEL = 1024
HEAD_DIM = 64
N_HEADS = 4
HEAD_W = 2 * HEAD_DIM
QK_W = N_HEADS * HEAD_W
V_W = QK_W
F_W = 512
F_GROUP = 64
D_FF = 2816
N_MOD = 9
NUM_BUCKETS = 32
REL_MAX_DISTANCE = 128
EPS = 1e-6
ATTN_SCALE = HEAD_DIM ** -0.5

V7X_VMEM_LIMIT_BYTES = 58 * 1024 * 1024
V7X_LANES = 128
V7X_MXU_DIM = 256
MOD_ROWS = 16
MOD_TN = 2304
FFN_TM = 1024
FFN_SUB = 512
FFN_CK = V7X_MXU_DIM
MIX_TM = 1024
MIX_SUB = 512
ATT_TQ = 512
ATT_TK = V7X_MXU_DIM
ATT_LOGIT_SCRATCH_BYTES = 32 * 1024 * 1024
N_BIAS_TILES = ATT_TQ // ATT_TK + 4
BIAS_ROWS = 128
LOG2E = math.log2(math.e)
OUT_TM = V7X_MXU_DIM
MIXOUT_SEQ_INPUT_BYTES = 8 * 1024 * 1024
DFT_B = 64
DFT_A_PER_STEP = 4
DFT_CLASSES = (1, 3, 0, 2)

f32 = jnp.float32
bf16 = jnp.bfloat16


def _dot(a, b):
    return jnp.dot(a, b, preferred_element_type=f32)


def _params(sem, vmem=V7X_VMEM_LIMIT_BYTES):
    return pltpu.CompilerParams(dimension_semantics=sem, vmem_limit_bytes=vmem)


def _resident(shape):
    return pl.BlockSpec(shape, lambda *_: (0,) * len(shape), pipeline_mode=pl.Buffered(1))


def _layer(shape, l):
    return pl.BlockSpec((None,) + tuple(shape), lambda *_: (l,) + (0,) * len(shape), pipeline_mode=pl.Buffered(1))


def _split_bf16(a):
    hi = a.astype(bf16)
    lo = (a - hi.astype(f32)).astype(bf16)
    return hi, lo


def _mod_norm(x, gain_scale, shift):
    ms = jnp.mean(x * x, axis=-1, keepdims=True)
    return (x * lax.rsqrt(ms + EPS)) * gain_scale + shift


def _mod_kernel(c_ref, w_ref, b_ref, o_ref):
    c = c_ref[...]
    sc = c * (1.0 / (1.0 + jnp.exp(-c)))
    s_hi, s_lo = _split_bf16(sc)
    w_hi, w_lo = _split_bf16(w_ref[0])
    o_ref[0] = _dot(s_hi, w_hi) + _dot(s_hi, w_lo) + _dot(s_lo, w_hi) + b_ref[0]


def _mod_call(c_all, ada_w, ada_b):
    depth, _, n = ada_w.shape
    return pl.pallas_call(
        _mod_kernel,
        grid=(depth, n // MOD_TN),
        in_specs=[
            pl.BlockSpec((MOD_ROWS, D_MODEL), lambda l, j: (0, 0)),
            pl.BlockSpec((1, D_MODEL, MOD_TN), lambda l, j: (l, 0, j)),
            pl.BlockSpec((1, 1, MOD_TN), lambda l, j: (l, 0, j)),
        ],
        out_specs=pl.BlockSpec((1, MOD_ROWS, MOD_TN), lambda l, j: (l, 0, j)),
        out_shape=jax.ShapeDtypeStruct((depth, MOD_ROWS, n), f32),
        compiler_params=_params(("arbitrary", "arbitrary")),
        name="adaln_mod",
    )(c_all, ada_w, ada_b.reshape(depth, 1, n))


def _ffn_kernel(x_ref, mod_ref, g_ref, wi_ref, wo_ref, o_ref, acc_ref, *, row0):
    shift, scale, gate = (mod_ref[0, 0, row0 + i:row0 + i + 1, :] for i in range(3))
    subs = [slice(r, r + FFN_SUB) for r in range(0, x_ref.shape[1], FFN_SUB)]
    gain_scale = g_ref[...] * (1.0 + scale)
    hbs = [_mod_norm(x_ref[0, rows, :], gain_scale, shift).astype(bf16) for rows in subs]
    for c in range(D_FF // FFN_CK):
        for rows, hb in zip(subs, hbs):
            g = _dot(hb, wi_ref[:, c * FFN_CK:(c + 1) * FFN_CK])
            u = _dot(hb, wi_ref[:, D_FF + c * FFN_CK:D_FF + (c + 1) * FFN_CK])
            a = (g * (1.0 / (1.0 + jnp.exp(-g))) * u).astype(bf16)
            part = _dot(a, wo_ref[c * FFN_CK:(c + 1) * FFN_CK, :])
            if c == 0:
                acc_ref[rows, :] = part
            else:
                acc_ref[rows, :] += part
    for rows in subs:
        o_ref[0, rows, :] = x_ref[0, rows, :] + (0.5 * gate) * acc_ref[rows, :]


def _mod_spec(l, row0):
    return pl.BlockSpec((1, 1, N_MOD, D_MODEL), lambda i, *_: (l, row0 + i, 0, 0))


def _ffn_call(x, mods, mrow, l, gain, wi, wo, row0):
    b, s, d = x.shape
    tm = min(FFN_TM, s)
    return pl.pallas_call(
        functools.partial(_ffn_kernel, row0=row0),
        grid=(b, s // tm),
        in_specs=[
            pl.BlockSpec((1, tm, d), lambda i, j: (i, j, 0)),
            _mod_spec(l, mrow),
            _layer((1, d), l),
            _layer((d, 2 * D_FF), l),
            _layer((D_FF, d), l),
        ],
        out_specs=pl.BlockSpec((1, tm, d), lambda i, j: (i, j, 0)),
        out_shape=jax.ShapeDtypeStruct(x.shape, f32),
        scratch_shapes=[pltpu.VMEM((tm, d), f32)],
        compiler_params=_params(("parallel", "parallel")),
        name="ffn",
    )(x, mods, gain, wi, wo)


def _group_rms(z, gmean_ref, gain):
    outs = []
    for c in range(QK_W // V7X_MXU_DIM):
        zc = z[:, c * V7X_MXU_DIM:(c + 1) * V7X_MXU_DIM]
        ms = _dot((zc * zc).astype(bf16), gmean_ref[...])
        outs.append(zc * lax.rsqrt(ms + EPS))
    return jnp.concatenate(outs, axis=-1) * gain


def _mixin_kernel(x_ref, mod_ref, g_ref, win_ref, gq_ref, gk_ref, gmean_ref, dc_ref, ds_ref,
                  q_ref, kt_ref, v_ref, xc_ref, xs_ref):
    shift, gain_scale = mod_ref[0, 0, 3:4, :], g_ref[...] * (1.0 + mod_ref[0, 0, 4:5, :])
    for r0 in range(0, x_ref.shape[1], MIX_SUB):
        rows = slice(r0, r0 + MIX_SUB)
        hb = _mod_norm(x_ref[0, rows, :], gain_scale, shift).astype(bf16)
        zq = _dot(hb, win_ref[:, 0:QK_W])
        q = (_group_rms(zq, gmean_ref, gq_ref[...]) * (ATTN_SCALE * LOG2E)).astype(bf16)
        first = lax.broadcasted_iota(jnp.int32, q.shape, 1) % HEAD_W < HEAD_DIM
        zero = jnp.zeros_like(q)
        q_ref[0, 0, rows, :] = jnp.where(first, q, zero)
        q_ref[1, 0, rows, :] = jnp.where(first, zero, q)
        zk = _dot(hb, win_ref[:, QK_W:2 * QK_W])
        kt_ref[0, :, rows] = _group_rms(zk, gmean_ref, gk_ref[...]).T.astype(bf16)
        v = _dot(hb, win_ref[:, 2 * QK_W:2 * QK_W + V_W]).astype(bf16)
        ones = jnp.ones((MIX_SUB, HEAD_W), bf16)
        for h in range(N_HEADS):
            v_ref[0, rows, 2 * h * HEAD_W:(2 * h + 1) * HEAD_W] = v[:, h * HEAD_W:(h + 1) * HEAD_W]
            v_ref[0, rows, (2 * h + 1) * HEAD_W:(2 * h + 2) * HEAD_W] = ones
        fb = _dot(hb, win_ref[:, 2 * QK_W + V_W:]).astype(bf16)
        for c in range(F_W // V7X_MXU_DIM):
            cols = slice(c * V7X_MXU_DIM, (c + 1) * V7X_MXU_DIM)
            xc_ref[0, rows, cols] = _dot(fb[:, cols], dc_ref[...]).astype(bf16)
            xs_ref[0, rows, cols] = _dot(fb[:, cols], ds_ref[...]).astype(bf16)


def _mixin_call(x, mods, mrow, l, gain, w_in, gq, gk, gmean, dc, ds):
    b, s, d = x.shape
    tm = min(MIX_TM, s)
    row = lambda i, j: (i, j, 0)
    tok = lambda w: jax.ShapeDtypeStruct((b, s, w), bf16)
    return pl.pallas_call(
        _mixin_kernel,
        grid=(b, s // tm),
        in_specs=[
            pl.BlockSpec((1, tm, d), row),
            _mod_spec(l, mrow),
            _layer((1, d), l),
            _layer(w_in.shape[1:], l),
            _layer((1, QK_W), l),
            _layer((1, QK_W), l),
            _resident((V7X_MXU_DIM, V7X_MXU_DIM)),
            _resident((V7X_MXU_DIM, V7X_MXU_DIM)),
            _resident((V7X_MXU_DIM, V7X_MXU_DIM)),
        ],
        out_specs=[
            pl.BlockSpec((2, 1, tm, QK_W), lambda i, j: (0, i, j, 0)),
            pl.BlockSpec((1, QK_W, tm), lambda i, j: (i, 0, j)),
            pl.BlockSpec((1, tm, 2 * V_W), row),
            pl.BlockSpec((1, tm, F_W), row),
            pl.BlockSpec((1, tm, F_W), row),
        ],
        out_shape=[jax.ShapeDtypeStruct((2, b, s, QK_W), bf16), jax.ShapeDtypeStruct((b, QK_W, s), bf16),
                   tok(2 * V_W), tok(F_W), tok(F_W)],
        compiler_params=_params(("parallel", "parallel")),
        name="mix_in",
    )(x, mods, gain, w_in, gq, gk, gmean, dc, ds)


def _attn_kernel(lqk_ref, q_ref, kt_ref, v_ref, bias_ref, o_ref, s_scr, m_scr, *, n_tiles, n_sub, lambda_init):
    tq, tk = ATT_TQ, ATT_TK
    qt = pl.program_id(2) * n_sub

    def score_map(i, carry):
        sub, j = i // 2, i % 2
        qj = q_ref[j, 0, pl.ds(pl.multiple_of(sub * tq, tq), tq), :]
        m_run = jnp.full((tq, V7X_LANES), -jnp.inf, bf16)
        for kt in range(n_tiles):
            tile = jnp.clip(kt - (qt + sub) * (tq // tk) + 2, 0, N_BIAS_TILES - 1)
            sj = _dot(qj, kt_ref[0, :, kt * tk:(kt + 1) * tk]) + bias_ref[j, tile]
            s_scr[i, :, kt * tk:(kt + 1) * tk] = sj
            m_run = jnp.maximum(m_run, jnp.maximum(sj[:, :V7X_LANES], sj[:, V7X_LANES:]).astype(bf16))
        m_scr[i] = m_run.astype(f32)
        return carry

    lax.fori_loop(0, 2 * n_sub, score_map, 0)

    lqk = lqk_ref[...]
    lam = (jnp.exp(jnp.sum(lqk[0:1] * lqk[1:2], axis=-1, keepdims=True))
           - jnp.exp(jnp.sum(lqk[2:3] * lqk[3:4], axis=-1, keepdims=True)) + lambda_init)
    for sub in range(n_sub):
        heads = []
        for i in (2 * sub, 2 * sub + 1):
            m = jnp.max(m_scr[i], axis=-1, keepdims=True)
            p = jnp.exp2((s_scr[i] - m).astype(bf16))
            ov = _dot(p, v_ref[0])
            heads.append(ov[:, :HEAD_W] / ov[:, HEAD_W:])
        o_ref[0, sub * tq:(sub + 1) * tq, :] = (heads[0] - lam * heads[1]).astype(bf16)


def _attn_call(l, lqk, q, kt, vx, bias, lambda_init):
    _, b, s, _ = q.shape
    tq, tk = ATT_TQ, ATT_TK
    assert s % tk == 0 and s % tq == 0
    n_sub = max(1, min(s // tq, ATT_LOGIT_SCRATCH_BYTES // (2 * tq * s * 4)))
    rows = n_sub * tq
    return pl.pallas_call(
        functools.partial(_attn_kernel, n_tiles=s // tk, n_sub=n_sub, lambda_init=lambda_init),
        grid=(N_HEADS, b, s // rows),
        in_specs=[
            _layer((4, HEAD_DIM), l),
            pl.BlockSpec((2, 1, rows, HEAD_W), lambda h, i, j: (0, i, j, h)),
            pl.BlockSpec((1, HEAD_W, s), lambda h, i, j: (i, h, 0)),
            pl.BlockSpec((1, s, 2 * HEAD_W), lambda h, i, j: (i, 0, h)),
            pl.BlockSpec((2, N_BIAS_TILES, tq, tk), lambda h, i, j: (h, 0, 0, 0)),
        ],
        out_specs=pl.BlockSpec((1, rows, HEAD_W), lambda h, i, j: (i, j, h)),
        out_shape=jax.ShapeDtypeStruct((b, s, V_W), bf16),
        scratch_shapes=[pltpu.VMEM((2 * n_sub, tq, s), f32), pltpu.VMEM((2 * n_sub, tq, V7X_LANES), f32)],
        compiler_params=_params(("parallel", "parallel", "arbitrary")),
        name="diff_attn",
    )(lqk, q, kt, vx, bias)


def _mixout_kernel(x_ref, mod_ref, o_ref, g_ref, xc_ref, xs_ref, cmo_ref, smo_ref, cme_ref, sme_ref, wout_ref, y_ref,
                   odd_scr, even_scr, fo_scr, *, scale, quarter, lambda_init):
    @pl.when(pl.program_id(1) == 0)
    def _():
        for t, src in enumerate((xc_ref, xs_ref)):
            x0, x1, x2, x3 = (src[0, i * quarter:(i + 1) * quarter] for i in range(4))
            odd_scr[t, :quarter] = x0 - x2
            odd_scr[t, quarter:] = x1 - x3
            lo, hi = x0 + x2, x1 + x3
            even_scr[t, 0] = lo + hi
            even_scr[t, 1] = lo - hi

    tm = cmo_ref.shape[1]
    parts = [(c, _dot(cmo_ref[i], odd_scr[0]) - _dot(smo_ref[i], odd_scr[1])) for i, c in enumerate(DFT_CLASSES[:2])]
    parts += [(c, _dot(cme_ref[i], even_scr[0, i]) - _dot(sme_ref[i], even_scr[1, i]))
              for i, c in enumerate(DFT_CLASSES[2:])]
    for c, fo in parts:
        fo = fo * scale
        for ch in range(F_W // V7X_LANES):
            fo_scr[ch, pl.ds(c, tm, stride=4), :] = fo[:, ch * V7X_LANES:(ch + 1) * V7X_LANES]
    fo = jnp.concatenate([fo_scr[ch] for ch in range(F_W // V7X_LANES)], axis=-1).astype(bf16)
    heads = []
    for h in range(N_HEADS):
        oh = o_ref[0, :, h * HEAD_W:(h + 1) * HEAD_W].astype(f32)
        ms = jnp.mean(oh * oh, axis=-1, keepdims=True)
        heads.append(((oh * lax.rsqrt(ms + EPS) * g_ref[...]) * (1.0 - lambda_init)).astype(bf16))
    mix = _dot(jnp.concatenate(heads, axis=-1), wout_ref[0:V_W, :]) + _dot(fo, wout_ref[V_W:, :])
    y_ref[0] = x_ref[0] + mod_ref[0, 0, 5:6, :] * mix


def _mixout_call(x, mods, mrow, l, o, subln, xc, xs, cm, sm, w_out, lambda_init):
    b, s, d = x.shape
    quarter = s // 4
    tm = min(OUT_TM, quarter)
    row = lambda i, j: (i, j, 0)
    seq_buffers = 2 if 2 * 2 * s * F_W * 2 <= MIXOUT_SEQ_INPUT_BYTES else 1
    seq = pl.BlockSpec((1, s, F_W), lambda i, j: (i, 0, 0), pipeline_mode=pl.Buffered(seq_buffers))
    odd = pl.BlockSpec((2, tm, 2 * quarter), lambda i, j: (0, j, 0))
    even = pl.BlockSpec((2, tm, quarter), lambda i, j: (1, j, 0))
    return pl.pallas_call(
        functools.partial(_mixout_kernel, scale=1.0 / math.sqrt(F_GROUP * s), quarter=quarter,
                          lambda_init=lambda_init),
        grid=(b, quarter // tm),
        in_specs=[
            pl.BlockSpec((1, 4 * tm, d), row),
            _mod_spec(l, mrow),
            pl.BlockSpec((1, 4 * tm, V_W), row),
            _layer((1, HEAD_W), l),
            seq, seq, odd, odd, even, even,
            _layer(w_out.shape[1:], l),
        ],
        out_specs=pl.BlockSpec((1, 4 * tm, d), row),
        out_shape=jax.ShapeDtypeStruct(x.shape, f32),
        scratch_shapes=[pltpu.VMEM((2, 2 * quarter, F_W), bf16), pltpu.VMEM((2, 2, quarter, F_W), bf16),
                        pltpu.VMEM((F_W // V7X_LANES, 4 * tm, V7X_LANES), f32)],
        compiler_params=_params(("parallel", "arbitrary")),
        name="mix_out",
    )(x, mods, o, subln, xc, xs, cm, sm, cm, sm, w_out)


def _dft_kernel(ac_ref, as_ref, bc_ref, bs_ref, cm_ref, sm_ref):
    bc, bsn = bc_ref[0], bs_ref[0]
    for i in range(ac_ref.shape[0]):
        ac, asn = ac_ref[i], as_ref[i]
        cm_ref[0, i * DFT_B:(i + 1) * DFT_B, :] = (ac * bc - asn * bsn).astype(bf16)
        sm_ref[0, i * DFT_B:(i + 1) * DFT_B, :] = (asn * bc + ac * bsn).astype(bf16)


def _dft_tables(s):
    half, quarter = s // 2, s // 4
    pos = np.arange(half, dtype=np.int64)
    ang = lambda freq: 2.0 * np.pi * ((freq[..., None] * pos) % s).astype(np.float64) / s
    a = ang(np.arange(quarter // DFT_B, dtype=np.int64) * (4 * DFT_B))[:, None, :]
    b = ang(4 * np.arange(DFT_B, dtype=np.int64)[None, :] + np.asarray(DFT_CLASSES, dtype=np.int64)[:, None])
    tab = lambda m: jnp.asarray(m.astype(np.float32))
    return tab(np.cos(a)), tab(np.sin(a)), tab(np.cos(b)), tab(np.sin(b))


def _dft_call(s):
    half, quarter = s // 2, s // 4
    a_per_step = min(DFT_A_PER_STEP, quarter // DFT_B)
    rows = DFT_B * a_per_step
    assert quarter % rows == 0
    ac, asn, bc, bsn = _dft_tables(s)
    aspec = pl.BlockSpec((a_per_step, 1, half), lambda c, a: (a, 0, 0))
    bspec = pl.BlockSpec((1, DFT_B, half), lambda c, a: (c, 0, 0))
    out = pl.BlockSpec((1, rows, half), lambda c, a: (c, a, 0))
    return pl.pallas_call(
        _dft_kernel,
        grid=(len(DFT_CLASSES), quarter // rows),
        in_specs=[aspec, aspec, bspec, bspec],
        out_specs=[out, out],
        out_shape=[jax.ShapeDtypeStruct((len(DFT_CLASSES), quarter, half), bf16)] * 2,
        compiler_params=_params(("parallel", "parallel")),
        name="dft_matrix",
    )(ac, asn, bc, bsn)


def _rel_bucket_np(rel):
    nb = NUM_BUCKETS // 2
    max_exact = nb // 2
    ret = (rel > 0).astype(np.int32) * nb
    n = np.abs(rel)
    nf = np.maximum(n, 1).astype(np.float32)
    ratio = np.log(nf / np.float32(max_exact)) / np.float32(math.log(REL_MAX_DISTANCE / max_exact))
    large = max_exact + (ratio * np.float32(nb - max_exact)).astype(np.int32)
    large = np.minimum(large, nb - 1)
    return (ret + np.where(n < max_exact, n, large)).astype(np.int32)


def _bucket_tiles():
    qpos = np.arange(ATT_TQ)[:, None]
    kpos = np.arange(ATT_TK)[None, :]
    return np.stack([_rel_bucket_np((d * ATT_TK + kpos) - qpos) for d in range(-2, N_BIAS_TILES - 2)])


def _bias_kernel(table_ref, bucket_ref, o_ref):
    n_maps = o_ref.shape[0]

    def rows(r, carry):
        r0 = pl.multiple_of(r * BIAS_ROWS, BIAS_ROWS)
        for c in range(ATT_TK // V7X_LANES):
            cols = slice(c * V7X_LANES, (c + 1) * V7X_LANES)
            bucket = bucket_ref[0, pl.ds(r0, BIAS_ROWS), cols]
            for hj in range(n_maps):
                row = jnp.broadcast_to(table_ref[hj:hj + 1, :], bucket.shape)
                o_ref[hj, 0, pl.ds(r0, BIAS_ROWS), cols] = jnp.take_along_axis(row, bucket, axis=1) * LOG2E
        return carry

    lax.fori_loop(0, ATT_TQ // BIAS_ROWS, rows, 0)


def _bias_call(rel_bias):
    n_maps = 2 * N_HEADS
    table = jnp.pad(rel_bias.reshape(NUM_BUCKETS, n_maps).T, ((0, 0), (0, V7X_LANES - NUM_BUCKETS)))
    return pl.pallas_call(
        _bias_kernel,
        grid=(N_BIAS_TILES,),
        in_specs=[
            _resident((n_maps, V7X_LANES)),
            pl.BlockSpec((1, ATT_TQ, ATT_TK), lambda t: (t, 0, 0)),
        ],
        out_specs=pl.BlockSpec((n_maps, 1, ATT_TQ, ATT_TK), lambda t: (0, t, 0, 0)),
        out_shape=jax.ShapeDtypeStruct((n_maps, N_BIAS_TILES, ATT_TQ, ATT_TK), f32),
        compiler_params=_params(("arbitrary",)),
        name="rel_bias_tiles",
    )(table, jnp.asarray(_bucket_tiles()))


def _channel_dft_tables():
    idx = np.arange(F_GROUP)
    ang = 2.0 * np.pi * ((idx[:, None] * idx[None, :]) % F_GROUP) / F_GROUP
    eye = np.eye(V7X_MXU_DIM // F_GROUP)
    return (jnp.asarray(np.kron(eye, np.cos(ang)), dtype=bf16), jnp.asarray(np.kron(eye, np.sin(ang)), dtype=bf16))


def _group_mean_table():
    block = np.full((HEAD_DIM, HEAD_DIM), 1.0 / HEAD_DIM)
    return jnp.asarray(np.kron(np.eye(V7X_MXU_DIM // HEAD_DIM), block), dtype=bf16)


def _trunk(x, mods, mrow, p, bias, cm, sm, consts):
    gmean, dc, ds = consts
    depth = p["w_in"].shape[0]
    for l in range(depth):
        lambda_init = 0.8 - 0.6 * math.exp(-0.3 * l)
        x = _ffn_call(x, mods, mrow, l, p["norm_ffn1"], p["ffn1_wi"], p["ffn1_wo"], 0)
        q, kt, vx, xc, xs = _mixin_call(x, mods, mrow, l, p["norm_mix"], p["w_in"], p["q_norm"], p["k_norm"],
                                        gmean, dc, ds)
        o = _attn_call(l, p["lambda_qk"], q, kt, vx, bias, lambda_init)
        x = _mixout_call(x, mods, mrow, l, o, p["subln"], xc, xs, cm, sm, p["w_out"], lambda_init)
        x = _ffn_call(x, mods, mrow, l, p["norm_ffn2"], p["ffn2_wi"], p["ffn2_wo"], 6)
    return x


def kernel(x_prompt, x_sample, c_prompt, c_sample, ada_w, ada_b, norm_ffn1, norm_mix, norm_ffn2, ffn1_wi, ffn1_wo,
           ffn2_wi, ffn2_wo, w_in, w_out, q_norm, k_norm, lambda_qk, subln, rel_bias):
    depth = ada_w.shape[0]
    nb_p, nb_s = c_prompt.shape[0], c_sample.shape[0]
    assert nb_p + nb_s <= MOD_ROWS
    c_all = jnp.concatenate([c_prompt, c_sample, jnp.zeros((MOD_ROWS - nb_p - nb_s, D_MODEL), f32)], axis=0)
    mods = _mod_call(c_all, ada_w, ada_b).reshape(depth, MOD_ROWS, N_MOD, D_MODEL)

    row = lambda a: a.reshape(depth, 1, -1)
    tile_heads = lambda a: jnp.tile(a, (1, QK_W // HEAD_DIM)).reshape(depth, 1, QK_W)
    p = dict(
        norm_ffn1=row(norm_ffn1), norm_mix=row(norm_mix), norm_ffn2=row(norm_ffn2),
        ffn1_wi=ffn1_wi.astype(bf16), ffn1_wo=ffn1_wo.astype(bf16),
        ffn2_wi=ffn2_wi.astype(bf16), ffn2_wo=ffn2_wo.astype(bf16),
        w_in=w_in.astype(bf16), w_out=w_out.astype(bf16),
        q_norm=tile_heads(q_norm), k_norm=tile_heads(k_norm),
        lambda_qk=lambda_qk, subln=row(subln),
    )
    bias = _bias_call(rel_bias)
    consts = (_group_mean_table(),) + _channel_dft_tables()

    outs = []
    for x, mrow in ((x_prompt, 0), (x_sample, nb_p)):
        cm, sm = _dft_call(x.shape[1])
        outs.append(_trunk(x, mods, mrow, p, bias, cm, sm, consts))
    return tuple(outs)
```

```python
import functools
import math

import numpy as np
import jax
import jax.numpy as jnp
from jax import lax
from jax.experimental import pallas as pl
from jax.experimental.pallas import tpu as pltpu

D_MODEL = 1024
HEAD_DIM = 64
N_HEADS = 4
HEAD_W = 2 * HEAD_DIM
QK_W = N_HEADS * HEAD_W
V_W = QK_W
F_W = 512
F_GROUP = 64
D_FF = 2816
N_MOD = 9
NUM_BUCKETS = 32
REL_MAX_DISTANCE = 128
EPS = 1e-6
ATTN_SCALE = HEAD_DIM ** -0.5

V7X_VMEM_LIMIT_BYTES = 58 * 1024 * 1024
V7X_LANES = 128
V7X_MXU_DIM = 256
MOD_ROWS = 16
MOD_TN = 2304
FFN_TM = 1024
FFN_SUB = 512
FFN_CK = V7X_MXU_DIM
MIX_TM = 1024
MIX_SUB = 512
ATT_TQ = 512
ATT_TK = V7X_MXU_DIM
ATT_LOGIT_SCRATCH_BYTES = 32 * 1024 * 1024
N_BIAS_TILES = ATT_TQ // ATT_TK + 4
BIAS_ROWS = 128
LOG2E = math.log2(math.e)
OUT_TM = V7X_MXU_DIM
MIXOUT_SEQ_INPUT_BYTES = 8 * 1024 * 1024
DFT_B = 64
DFT_A_PER_STEP = 4
DFT_CLASSES = (1, 3, 0, 2)

f32 = jnp.float32
bf16 = jnp.bfloat16


def _dot(a, b):
    return jnp.dot(a, b, preferred_element_type=f32)


def _params(sem, vmem=V7X_VMEM_LIMIT_BYTES):
    return pltpu.CompilerParams(dimension_semantics=sem, vmem_limit_bytes=vmem)


def _resident(shape):
    return pl.BlockSpec(shape, lambda *_: (0,) * len(shape), pipeline_mode=pl.Buffered(1))


def _layer(shape, l):
    return pl.BlockSpec((None,) + tuple(shape), lambda *_: (l,) + (0,) * len(shape), pipeline_mode=pl.Buffered(1))


def _split_bf16(a):
    hi = a.astype(bf16)
    lo = (a - hi.astype(f32)).astype(bf16)
    return hi, lo


def _mod_norm(x, gain_scale, shift):
    ms = jnp.mean(x * x, axis=-1, keepdims=True)
    return (x * lax.rsqrt(ms + EPS)).astype(bf16) * gain_scale.astype(bf16) + shift.astype(bf16)


def _mod_kernel(c_ref, w_ref, b_ref, o_ref):
    c = c_ref[...]
    sc = c * (1.0 / (1.0 + jnp.exp(-c)))
    s_hi, s_lo = _split_bf16(sc)
    w_hi, w_lo = _split_bf16(w_ref[0])
    o_ref[0] = _dot(s_hi, w_hi) + _dot(s_hi, w_lo) + _dot(s_lo, w_hi) + b_ref[0]


def _mod_call(c_all, ada_w, ada_b):
    depth, _, n = ada_w.shape
    return pl.pallas_call(
        _mod_kernel,
        grid=(depth, n // MOD_TN),
        in_specs=[
            pl.BlockSpec((MOD_ROWS, D_MODEL), lambda l, j: (0, 0)),
            pl.BlockSpec((1, D_MODEL, MOD_TN), lambda l, j: (l, 0, j)),
            pl.BlockSpec((1, 1, MOD_TN), lambda l, j: (l, 0, j)),
        ],
        out_specs=pl.BlockSpec((1, MOD_ROWS, MOD_TN), lambda l, j: (l, 0, j)),
        out_shape=jax.ShapeDtypeStruct((depth, MOD_ROWS, n), f32),
        compiler_params=_params(("arbitrary", "arbitrary")),
        name="adaln_mod",
    )(c_all, ada_w, ada_b.reshape(depth, 1, n))


def _ffn_kernel(x_ref, mod_ref, g_ref, wi_ref, wo_ref, o_ref, acc_ref, *, row0):
    shift, scale, gate = (mod_ref[0, 0, row0 + i:row0 + i + 1, :] for i in range(3))
    subs = [slice(r, r + FFN_SUB) for r in range(0, x_ref.shape[1], FFN_SUB)]
    gain_scale = g_ref[...] * (1.0 + scale)
    hbs = [_mod_norm(x_ref[0, rows, :], gain_scale, shift).astype(bf16) for rows in subs]
    for c in range(D_FF // FFN_CK):
        for rows, hb in zip(subs, hbs):
            g = _dot(hb, wi_ref[:, c * FFN_CK:(c + 1) * FFN_CK])
            u = _dot(hb, wi_ref[:, D_FF + c * FFN_CK:D_FF + (c + 1) * FFN_CK])
            a = (g * (1.0 / (1.0 + jnp.exp(-g))) * u).astype(bf16)
            part = _dot(a, wo_ref[c * FFN_CK:(c + 1) * FFN_CK, :])
            if c == 0:
                acc_ref[rows, :] = part
            else:
                acc_ref[rows, :] += part
    for rows in subs:
        o_ref[0, rows, :] = x_ref[0, rows, :] + (0.5 * gate) * acc_ref[rows, :]


def _mod_spec(l, row0):
    return pl.BlockSpec((1, 1, N_MOD, D_MODEL), lambda i, *_: (l, row0 + i, 0, 0))


def _ffn_call(x, mods, mrow, l, gain, wi, wo, row0):
    b, s, d = x.shape
    tm = min(FFN_TM, s)
    return pl.pallas_call(
        functools.partial(_ffn_kernel, row0=row0),
        grid=(b, s // tm),
        in_specs=[
            pl.BlockSpec((1, tm, d), lambda i, j: (i, j, 0)),
            _mod_spec(l, mrow),
            _layer((1, d), l),
            _layer((d, 2 * D_FF), l),
            _layer((D_FF, d), l),
        ],
        out_specs=pl.BlockSpec((1, tm, d), lambda i, j: (i, j, 0)),
        out_shape=jax.ShapeDtypeStruct(x.shape, f32),
        scratch_shapes=[pltpu.VMEM((tm, d), f32)],
        compiler_params=_params(("parallel", "parallel")),
        name="ffn",
    )(x, mods, gain, wi, wo)


def _group_rms(z, gmean_ref, gain):
    outs = []
    for c in range(QK_W // V7X_MXU_DIM):
        zc = z[:, c * V7X_MXU_DIM:(c + 1) * V7X_MXU_DIM]
        ms = _dot((zc * zc).astype(bf16), gmean_ref[...])
        outs.append(zc * lax.rsqrt(ms + EPS))
    return jnp.concatenate(outs, axis=-1) * gain


def _mixin_kernel(x_ref, mod_ref, g_ref, win_ref, gq_ref, gk_ref, gmean_ref, dc_ref, ds_ref,
                  q_ref, kt_ref, v_ref, xc_ref, xs_ref):
    shift, gain_scale = mod_ref[0, 0, 3:4, :], g_ref[...] * (1.0 + mod_ref[0, 0, 4:5, :])
    for r0 in range(0, x_ref.shape[1], MIX_SUB):
        rows = slice(r0, r0 + MIX_SUB)
        hb = _mod_norm(x_ref[0, rows, :], gain_scale, shift).astype(bf16)
        zq = _dot(hb, win_ref[:, 0:QK_W])
        q = (_group_rms(zq, gmean_ref, gq_ref[...]) * (ATTN_SCALE * LOG2E)).astype(bf16)
        first = lax.broadcasted_iota(jnp.int32, q.shape, 1) % HEAD_W < HEAD_DIM
        zero = jnp.zeros_like(q)
        q_ref[0, 0, rows, :] = jnp.where(first, q, zero)
        q_ref[1, 0, rows, :] = jnp.where(first, zero, q)
        zk = _dot(hb, win_ref[:, QK_W:2 * QK_W])
        kt_ref[0, :, rows] = _group_rms(zk, gmean_ref, gk_ref[...]).T.astype(bf16)
        v = _dot(hb, win_ref[:, 2 * QK_W:2 * QK_W + V_W]).astype(bf16)
        ones = jnp.ones((MIX_SUB, HEAD_W), bf16)
        for h in range(N_HEADS):
            v_ref[0, rows, 2 * h * HEAD_W:(2 * h + 1) * HEAD_W] = v[:, h * HEAD_W:(h + 1) * HEAD_W]
            v_ref[0, rows, (2 * h + 1) * HEAD_W:(2 * h + 2) * HEAD_W] = ones
        fb = _dot(hb, win_ref[:, 2 * QK_W + V_W:]).astype(bf16)
        for c in range(F_W // V7X_MXU_DIM):
            cols = slice(c * V7X_MXU_DIM, (c + 1) * V7X_MXU_DIM)
            xc_ref[0, rows, cols] = _dot(fb[:, cols], dc_ref[...]).astype(bf16)
            xs_ref[0, rows, cols] = _dot(fb[:, cols], ds_ref[...]).astype(bf16)


def _mixin_call(x, mods, mrow, l, gain, w_in, gq, gk, gmean, dc, ds):
    b, s, d = x.shape
    tm = min(MIX_TM, s)
    row = lambda i, j: (i, j, 0)
    tok = lambda w: jax.ShapeDtypeStruct((b, s, w), bf16)
    return pl.pallas_call(
        _mixin_kernel,
        grid=(b, s // tm),
        in_specs=[
            pl.BlockSpec((1, tm, d), row),
            _mod_spec(l, mrow),
            _layer((1, d), l),
            _layer(w_in.shape[1:], l),
            _layer((1, QK_W), l),
            _layer((1, QK_W), l),
            _resident((V7X_MXU_DIM, V7X_MXU_DIM)),
            _resident((V7X_MXU_DIM, V7X_MXU_DIM)),
            _resident((V7X_MXU_DIM, V7X_MXU_DIM)),
        ],
        out_specs=[
            pl.BlockSpec((2, 1, tm, QK_W), lambda i, j: (0, i, j, 0)),
            pl.BlockSpec((1, QK_W, tm), lambda i, j: (i, 0, j)),
            pl.BlockSpec((1, tm, 2 * V_W), row),
            pl.BlockSpec((1, tm, F_W), row),
            pl.BlockSpec((1, tm, F_W), row),
        ],
        out_shape=[jax.ShapeDtypeStruct((2, b, s, QK_W), bf16), jax.ShapeDtypeStruct((b, QK_W, s), bf16),
                   tok(2 * V_W), tok(F_W), tok(F_W)],
        compiler_params=_params(("parallel", "parallel")),
        name="mix_in",
    )(x, mods, gain, w_in, gq, gk, gmean, dc, ds)


def _attn_kernel(lqk_ref, q_ref, kt_ref, v_ref, bias_ref, o_ref, s_scr, m_scr, *, n_tiles, n_sub, lambda_init):
    tq, tk = ATT_TQ, ATT_TK
    qt = pl.program_id(2) * n_sub

    def score_map(i, carry):
        sub, j = i // 2, i % 2
        qj = q_ref[j, 0, pl.ds(pl.multiple_of(sub * tq, tq), tq), :]
        m_run = jnp.full((tq, V7X_LANES), -jnp.inf, bf16)
        for kt in range(n_tiles):
            tile = jnp.clip(kt - (qt + sub) * (tq // tk) + 2, 0, N_BIAS_TILES - 1)
            sj = _dot(qj, kt_ref[0, :, kt * tk:(kt + 1) * tk]) + bias_ref[j, tile]
            s_scr[i, :, kt * tk:(kt + 1) * tk] = sj
            m_run = jnp.maximum(m_run, jnp.maximum(sj[:, :V7X_LANES], sj[:, V7X_LANES:]).astype(bf16))
        m_scr[i] = m_run.astype(f32)
        return carry

    lax.fori_loop(0, 2 * n_sub, score_map, 0)

    lqk = lqk_ref[...]
    lam = (jnp.exp(jnp.sum(lqk[0:1] * lqk[1:2], axis=-1, keepdims=True))
           - jnp.exp(jnp.sum(lqk[2:3] * lqk[3:4], axis=-1, keepdims=True)) + lambda_init)
    for sub in range(n_sub):
        heads = []
        for i in (2 * sub, 2 * sub + 1):
            m = jnp.max(m_scr[i], axis=-1, keepdims=True)
            p = jnp.exp2((s_scr[i] - m).astype(bf16))
            ov = _dot(p, v_ref[0])
            heads.append(ov[:, :HEAD_W] / ov[:, HEAD_W:])
        o_ref[0, sub * tq:(sub + 1) * tq, :] = (heads[0] - lam * heads[1]).astype(bf16)


def _attn_call(l, lqk, q, kt, vx, bias, lambda_init):
    _, b, s, _ = q.shape
    tq, tk = ATT_TQ, ATT_TK
    assert s % tk == 0 and s % tq == 0
    n_sub = max(1, min(s // tq, ATT_LOGIT_SCRATCH_BYTES // (2 * tq * s * 4)))
    rows = n_sub * tq
    return pl.pallas_call(
        functools.partial(_attn_kernel, n_tiles=s // tk, n_sub=n_sub, lambda_init=lambda_init),
        grid=(b, N_HEADS, s // rows),
        in_specs=[
            _layer((4, HEAD_DIM), l),
            pl.BlockSpec((2, 1, rows, HEAD_W), lambda i, h, j: (0, i, j, h)),
            pl.BlockSpec((1, HEAD_W, s), lambda i, h, j: (i, h, 0)),
            pl.BlockSpec((1, s, 2 * HEAD_W), lambda i, h, j: (i, 0, h)),
            pl.BlockSpec((2, N_BIAS_TILES, tq, tk), lambda i, h, j: (h, 0, 0, 0)),
        ],
        out_specs=pl.BlockSpec((1, rows, HEAD_W), lambda i, h, j: (i, j, h)),
        out_shape=jax.ShapeDtypeStruct((b, s, V_W), bf16),
        scratch_shapes=[pltpu.VMEM((2 * n_sub, tq, s), f32), pltpu.VMEM((2 * n_sub, tq, V7X_LANES), f32)],
        compiler_params=_params(("parallel", "parallel", "arbitrary")),
        name="diff_attn",
    )(lqk, q, kt, vx, bias)


def _mixout_kernel(x_ref, mod_ref, o_ref, g_ref, xc_ref, xs_ref, cmo_ref, smo_ref, cme_ref, sme_ref, wout_ref, y_ref,
                   odd_scr, even_scr, fo_scr, *, scale, quarter, lambda_init):
    @pl.when(pl.program_id(1) == 0)
    def _():
        for t, src in enumerate((xc_ref, xs_ref)):
            x0, x1, x2, x3 = (src[0, i * quarter:(i + 1) * quarter] for i in range(4))
            odd_scr[t, :quarter] = x0 - x2
            odd_scr[t, quarter:] = x1 - x3
            lo, hi = x0 + x2, x1 + x3
            even_scr[t, 0] = lo + hi
            even_scr[t, 1] = lo - hi

    tm = cmo_ref.shape[1]
    parts = [(c, _dot(cmo_ref[i], odd_scr[0]) - _dot(smo_ref[i], odd_scr[1])) for i, c in enumerate(DFT_CLASSES[:2])]
    parts += [(c, _dot(cme_ref[i], even_scr[0, i]) - _dot(sme_ref[i], even_scr[1, i]))
              for i, c in enumerate(DFT_CLASSES[2:])]
    for c, fo in parts:
        fo = fo * scale
        for ch in range(F_W // V7X_LANES):
            fo_scr[ch, pl.ds(c, tm, stride=4), :] = fo[:, ch * V7X_LANES:(ch + 1) * V7X_LANES]
    fo = jnp.concatenate([fo_scr[ch] for ch in range(F_W // V7X_LANES)], axis=-1).astype(bf16)
    heads = []
    for h in range(N_HEADS):
        oh = o_ref[0, :, h * HEAD_W:(h + 1) * HEAD_W].astype(f32)
        ms = jnp.mean(oh * oh, axis=-1, keepdims=True)
        heads.append(((oh * lax.rsqrt(ms + EPS) * g_ref[...]) * (1.0 - lambda_init)).astype(bf16))
    mix = _dot(jnp.concatenate(heads, axis=-1), wout_ref[0:V_W, :]) + _dot(fo, wout_ref[V_W:, :])
    y_ref[0] = x_ref[0] + mod_ref[0, 0, 5:6, :] * mix


def _mixout_call(x, mods, mrow, l, o, subln, xc, xs, cm, sm, w_out, lambda_init):
    b, s, d = x.shape
    quarter = s // 4
    tm = min(OUT_TM, quarter)
    row = lambda i, j: (i, j, 0)
    seq_buffers = 2 if 2 * 2 * s * F_W * 2 <= MIXOUT_SEQ_INPUT_BYTES else 1
    seq = pl.BlockSpec((1, s, F_W), lambda i, j: (i, 0, 0), pipeline_mode=pl.Buffered(seq_buffers))
    odd = pl.BlockSpec((2, tm, 2 * quarter), lambda i, j: (0, j, 0))
    even = pl.BlockSpec((2, tm, quarter), lambda i, j: (1, j, 0))
    return pl.pallas_call(
        functools.partial(_mixout_kernel, scale=1.0 / math.sqrt(F_GROUP * s), quarter=quarter,
                          lambda_init=lambda_init),
        grid=(b, quarter // tm),
        in_specs=[
            pl.BlockSpec((1, 4 * tm, d), row),
            _mod_spec(l, mrow),
            pl.BlockSpec((1, 4 * tm, V_W), row),
            _layer((1, HEAD_W), l),
            seq, seq, odd, odd, even, even,
            _layer(w_out.shape[1:], l),
        ],
        out_specs=pl.BlockSpec((1, 4 * tm, d), row),
        out_shape=jax.ShapeDtypeStruct(x.shape, f32),
        scratch_shapes=[pltpu.VMEM((2, 2 * quarter, F_W), bf16), pltpu.VMEM((2, 2, quarter, F_W), bf16),
                        pltpu.VMEM((F_W // V7X_LANES, 4 * tm, V7X_LANES), f32)],
        compiler_params=_params(("parallel", "arbitrary")),
        name="mix_out",
    )(x, mods, o, subln, xc, xs, cm, sm, cm, sm, w_out)


def _dft_kernel(ac_ref, as_ref, bc_ref, bs_ref, cm_ref, sm_ref):
    bc, bsn = bc_ref[0], bs_ref[0]
    for i in range(ac_ref.shape[0]):
        ac, asn = ac_ref[i], as_ref[i]
        cm_ref[0, i * DFT_B:(i + 1) * DFT_B, :] = (ac * bc - asn * bsn).astype(bf16)
        sm_ref[0, i * DFT_B:(i + 1) * DFT_B, :] = (asn * bc + ac * bsn).astype(bf16)


def _dft_tables(s):
    half, quarter = s // 2, s // 4
    pos = np.arange(half, dtype=np.int64)
    ang = lambda freq: 2.0 * np.pi * ((freq[..., None] * pos) % s).astype(np.float64) / s
    a = ang(np.arange(quarter // DFT_B, dtype=np.int64) * (4 * DFT_B))[:, None, :]
    b = ang(4 * np.arange(DFT_B, dtype=np.int64)[None, :] + np.asarray(DFT_CLASSES, dtype=np.int64)[:, None])
    tab = lambda m: jnp.asarray(m.astype(np.float32))
    return tab(np.cos(a)), tab(np.sin(a)), tab(np.cos(b)), tab(np.sin(b))


def _dft_call(s):
    half, quarter = s // 2, s // 4
    a_per_step = min(DFT_A_PER_STEP, quarter // DFT_B)
    rows = DFT_B * a_per_step
    assert quarter % rows == 0
    ac, asn, bc, bsn = _dft_tables(s)
    aspec = pl.BlockSpec((a_per_step, 1, half), lambda c, a: (a, 0, 0))
    bspec = pl.BlockSpec((1, DFT_B, half), lambda c, a: (c, 0, 0))
    out = pl.BlockSpec((1, rows, half), lambda c, a: (c, a, 0))
    return pl.pallas_call(
        _dft_kernel,
        grid=(len(DFT_CLASSES), quarter // rows),
        in_specs=[aspec, aspec, bspec, bspec],
        out_specs=[out, out],
        out_shape=[jax.ShapeDtypeStruct((len(DFT_CLASSES), quarter, half), bf16)] * 2,
        compiler_params=_params(("parallel", "parallel")),
        name="dft_matrix",
    )(ac, asn, bc, bsn)


def _rel_bucket_np(rel):
    nb = NUM_BUCKETS // 2
    max_exact = nb // 2
    ret = (rel > 0).astype(np.int32) * nb
    n = np.abs(rel)
    nf = np.maximum(n, 1).astype(np.float32)
    ratio = np.log(nf / np.float32(max_exact)) / np.float32(math.log(REL_MAX_DISTANCE / max_exact))
    large = max_exact + (ratio * np.float32(nb - max_exact)).astype(np.int32)
    large = np.minimum(large, nb - 1)
    return (ret + np.where(n < max_exact, n, large)).astype(np.int32)


def _bucket_tiles():
    qpos = np.arange(ATT_TQ)[:, None]
    kpos = np.arange(ATT_TK)[None, :]
    return np.stack([_rel_bucket_np((d * ATT_TK + kpos) - qpos) for d in range(-2, N_BIAS_TILES - 2)])


def _bias_kernel(table_ref, bucket_ref, o_ref):
    n_maps = o_ref.shape[0]

    def rows(r, carry):
        r0 = pl.multiple_of(r * BIAS_ROWS, BIAS_ROWS)
        for c in range(ATT_TK // V7X_LANES):
            cols = slice(c * V7X_LANES, (c + 1) * V7X_LANES)
            bucket = bucket_ref[0, pl.ds(r0, BIAS_ROWS), cols]
            for hj in range(n_maps):
                row = jnp.broadcast_to(table_ref[hj:hj + 1, :], bucket.shape)
                o_ref[hj, 0, pl.ds(r0, BIAS_ROWS), cols] = jnp.take_along_axis(row, bucket, axis=1) * LOG2E
        return carry

    lax.fori_loop(0, ATT_TQ // BIAS_ROWS, rows, 0)


def _bias_call(rel_bias):
    n_maps = 2 * N_HEADS
    table = jnp.pad(rel_bias.reshape(NUM_BUCKETS, n_maps).T, ((0, 0), (0, V7X_LANES - NUM_BUCKETS)))
    return pl.pallas_call(
        _bias_kernel,
        grid=(N_BIAS_TILES,),
        in_specs=[
            _resident((n_maps, V7X_LANES)),
            pl.BlockSpec((1, ATT_TQ, ATT_TK), lambda t: (t, 0, 0)),
        ],
        out_specs=pl.BlockSpec((n_maps, 1, ATT_TQ, ATT_TK), lambda t: (0, t, 0, 0)),
        out_shape=jax.ShapeDtypeStruct((n_maps, N_BIAS_TILES, ATT_TQ, ATT_TK), f32),
        compiler_params=_params(("arbitrary",)),
        name="rel_bias_tiles",
    )(table, jnp.asarray(_bucket_tiles()))


def _channel_dft_tables():
    idx = np.arange(F_GROUP)
    ang = 2.0 * np.pi * ((idx[:, None] * idx[None, :]) % F_GROUP) / F_GROUP
    eye = np.eye(V7X_MXU_DIM // F_GROUP)
    return (jnp.asarray(np.kron(eye, np.cos(ang)), dtype=bf16), jnp.asarray(np.kron(eye, np.sin(ang)), dtype=bf16))


def _group_mean_table():
    block = np.full((HEAD_DIM, HEAD_DIM), 1.0 / HEAD_DIM)
    return jnp.asarray(np.kron(np.eye(V7X_MXU_DIM // HEAD_DIM), block), dtype=bf16)


def _trunk(x, mods, mrow, p, bias, cm, sm, consts):
    gmean, dc, ds = consts
    depth = p["w_in"].shape[0]
    for l in range(depth):
        lambda_init = 0.8 - 0.6 * math.exp(-0.3 * l)
        x = _ffn_call(x, mods, mrow, l, p["norm_ffn1"], p["ffn1_wi"], p["ffn1_wo"], 0)
        q, kt, vx, xc, xs = _mixin_call(x, mods, mrow, l, p["norm_mix"], p["w_in"], p["q_norm"], p["k_norm"],
                                        gmean, dc, ds)
        o = _attn_call(l, p["lambda_qk"], q, kt, vx, bias, lambda_init)
        x = _mixout_call(x, mods, mrow, l, o, p["subln"], xc, xs, cm, sm, p["w_out"], lambda_init)
        x = _ffn_call(x, mods, mrow, l, p["norm_ffn2"], p["ffn2_wi"], p["ffn2_wo"], 6)
    return x


def kernel(x_prompt, x_sample, c_prompt, c_sample, ada_w, ada_b, norm_ffn1, norm_mix, norm_ffn2, ffn1_wi, ffn1_wo,
           ffn2_wi, ffn2_wo, w_in, w_out, q_norm, k_norm, lambda_qk, subln, rel_bias):
    depth = ada_w.shape[0]
    nb_p, nb_s = c_prompt.shape[0], c_sample.shape[0]
    assert nb_p + nb_s <= MOD_ROWS
    c_all = jnp.concatenate([c_prompt, c_sample, jnp.zeros((MOD_ROWS - nb_p - nb_s, D_MODEL), f32)], axis=0)
    mods = _mod_call(c_all, ada_w, ada_b).reshape(depth, MOD_ROWS, N_MOD, D_MODEL)

    row = lambda a: a.reshape(depth, 1, -1)
    tile_heads = lambda a: jnp.tile(a, (1, QK_W // HEAD_DIM)).reshape(depth, 1, QK_W)
    p = dict(
        norm_ffn1=row(norm_ffn1), norm_mix=row(norm_mix), norm_ffn2=row(norm_ffn2),
        ffn1_wi=ffn1_wi.astype(bf16), ffn1_wo=ffn1_wo.astype(bf16),
        ffn2_wi=ffn2_wi.astype(bf16), ffn2_wo=ffn2_wo.astype(bf16),
        w_in=w_in.astype(bf16), w_out=w_out.astype(bf16),
        q_norm=tile_heads(q_norm), k_norm=tile_heads(k_norm),
        lambda_qk=lambda_qk, subln=row(subln),
    )
    bias = _bias_call(rel_bias)
    consts = (_group_mean_table(),) + _channel_dft_tables()

    outs = []
    for x, mrow in ((x_prompt, 0), (x_sample, nb_p)):
        cm, sm = _dft_call(x.shape[1])
        outs.append(_trunk(x, mods, mrow, p, bias, cm, sm, consts))
    return tuple(outs)
```
